```python
import jax, jax.numpy as jnp
from jax import lax
import numpy as np

D_MODEL = 2048
BATCH = 2
SEQ = 8192
DEPTH = 1

N_HEADS_MLA = 8
Q_LORA = 512
KV_LORA = 256
MLA_NOPE = 128
MLA_ROPE = 64
MLA_V = 128
MLA_QK = MLA_NOPE + MLA_ROPE
DIL_GROUPS = ((128, 1), (512, 4), (2048, 16))
N_DIL_GROUPS = len(DIL_GROUPS)
N_HEADS_DIL = 8
DIL_HEAD = 128
DIL_ROT = DIL_HEAD // 4
ROPE_THETA = 500000.0
D_FF = 5632
PLE_DIM = 256
EPS = 1e-6
Q_BLOCK = 128
NEG = -1e30

OFF_CQ = 0
OFF_CKV = OFF_CQ + Q_LORA
OFF_KR = OFF_CKV + KV_LORA
OFF_DIL = OFF_KR + MLA_ROPE
DIL_QKV = N_DIL_GROUPS * 3 * N_HEADS_DIL * DIL_HEAD
OFF_GATE = OFF_DIL + DIL_QKV
N_BRANCH = 2
D_IN = OFF_GATE + N_BRANCH * D_MODEL

kernel_name = "hybrid_mla_dilated_gated_macaron"


def rmsnorm(x, g):
    xf = x.astype(jnp.float32)
    y = xf * lax.rsqrt(jnp.mean(xf * xf, axis=-1, keepdims=True) + EPS)
    return (y * g.astype(jnp.float32)).astype(x.dtype)


def rope(x, pos):
    rd = x.shape[-1]
    half = rd // 2
    inv = ROPE_THETA ** (-jnp.arange(half, dtype=jnp.float32) * 2.0 / rd)
    ang = pos.astype(jnp.float32)[..., None] * inv
    cos = jnp.cos(ang)[:, :, None, :]
    sin = jnp.sin(ang)[:, :, None, :]
    xf = x.astype(jnp.float32)
    x1, x2 = xf[..., :half], xf[..., half:]
    return jnp.concatenate([x1 * cos - x2 * sin, x2 * cos + x1 * sin], axis=-1).astype(x.dtype)


def swiglu(h, wg, wu, wd):
    return (jax.nn.silu(h @ wg) * (h @ wu)) @ wd


def causal_block_attention(q, k, v):
    B, S, H, Dq = q.shape
    nq = S // Q_BLOCK
    qb = q.reshape(B, nq, Q_BLOCK, H, Dq).transpose(1, 0, 2, 3, 4)
    starts = jnp.arange(nq, dtype=jnp.int32) * Q_BLOCK
    kpos = jnp.arange(S, dtype=jnp.int32)

    def one_block(args):
        qi, s0 = args
        sc = jnp.einsum('bqhd,bkhd->bhqk', qi, k).astype(jnp.float32)
        qpos = s0 + jnp.arange(Q_BLOCK, dtype=jnp.int32)
        mask = kpos[None, :] <= qpos[:, None]
        pr = jax.nn.softmax(jnp.where(mask[None, None], sc, NEG), axis=-1)
        return jnp.einsum('bhqk,bkhd->bqhd', pr.astype(v.dtype), v)

    out = lax.map(one_block, (qb, starts))
    return out.transpose(1, 0, 2, 3, 4).reshape(B, S, H, v.shape[-1])


def dilated_window_attention(q, k, v, dil, n):
    B, S, H, Dh = q.shape
    L = S // dil
    Lp = -(-L // n) * n
    nb = Lp // n

    def phase_major(t):
        t = t.reshape(B, L, dil, H, Dh).transpose(0, 2, 1, 3, 4)
        return jnp.pad(t, ((0, 0), (0, 0), (0, Lp - L), (0, 0), (0, 0)))

    def windows(t):
        t = jnp.pad(phase_major(t), ((0, 0), (0, 0), (n, 0), (0, 0), (0, 0)))
        t = t.reshape(B, dil, nb + 1, n, H, Dh)
        return jnp.concatenate([t[:, :, :-1], t[:, :, 1:]], axis=3)

    qb = phase_major(q).reshape(B, dil, nb, n, H, Dh)
    kw = windows(k)
    vw = windows(v)
    sc = jnp.einsum('brnqhc,brnkhc->brnhqk', qb, kw).astype(jnp.float32)
    i = jnp.arange(n)[:, None]
    j = jnp.arange(2 * n)[None, :]
    dist = i + n - j
    band = (dist >= 0) & (dist <= n)
    keyok = ((jnp.arange(nb)[:, None] - 1) * n + jnp.arange(2 * n)[None, :]) >= 0
    valid = (band[None] & keyok[:, None, :])[None, None, :, None]
    m = jnp.max(jnp.where(valid, sc, NEG), axis=-1)
    e = jnp.where(valid, jnp.exp(sc - m[..., None]), 0.0)
    l = jnp.sum(e, axis=-1)
    o = jnp.einsum('brnhqk,brnkhc->brnqhc', (e / l[..., None]).astype(v.dtype), vw)
    o = o.reshape(B, dil, Lp, H, Dh)[:, :, :L].transpose(0, 2, 1, 3, 4).reshape(B, S, H, Dh)

    def back(s):
        s = s.transpose(0, 1, 2, 4, 3).reshape(B, dil, Lp, H)[:, :, :L]
        return s.transpose(0, 2, 1, 3).reshape(B, S, H)

    return o, back(m), back(l)


def setup_inputs(seed: int = 0) -> dict:
    key = jax.random.key(seed)
    ks = jax.random.split(key, 32)
    f32 = jnp.float32

    def nrm(k, shape, fan):
        return jax.random.normal(k, shape, f32) * fan ** -0.5

    def gain(k, shape):
        return 1.0 + 0.1 * jax.random.normal(k, shape, f32)

    Dp = DEPTH
    return {
        "x": jax.random.normal(ks[0], (BATCH, SEQ, D_MODEL), f32),
        "p": jax.random.normal(ks[1], (DEPTH, BATCH, SEQ, PLE_DIM), f32),
        "positions": jnp.arange(SEQ, dtype=jnp.int32)[None, :]
        + jax.random.randint(ks[2], (BATCH, 1), 0, 4096, dtype=jnp.int32),
        "g_ffn1": gain(ks[3], (Dp, D_MODEL)),
        "w1_gate": nrm(ks[4], (Dp, D_MODEL, D_FF), D_MODEL),
        "w1_up": nrm(ks[5], (Dp, D_MODEL, D_FF), D_MODEL),
        "w1_down": nrm(ks[6], (Dp, D_FF, D_MODEL), D_FF),
        "g_mix": gain(ks[7], (Dp, D_MODEL)),
        "w_in": nrm(ks[8], (Dp, D_MODEL, D_IN), D_MODEL),
        "g_cq": gain(ks[9], (Dp, Q_LORA)),
        "w_uq": nrm(ks[10], (Dp, Q_LORA, N_HEADS_MLA * MLA_QK), Q_LORA),
        "g_ckv": gain(ks[11], (Dp, KV_LORA)),
        "w_ukv": nrm(ks[12], (Dp, KV_LORA, N_HEADS_MLA * (MLA_NOPE + MLA_V)), KV_LORA),
        "g_q_mla": gain(ks[13], (Dp, MLA_QK)),
        "g_k_mla": gain(ks[14], (Dp, MLA_QK)),
        "g_q_dil": gain(ks[15], (Dp, N_DIL_GROUPS, DIL_HEAD)),
        "g_k_dil": gain(ks[16], (Dp, N_DIL_GROUPS, DIL_HEAD)),
        "w_br_mla": nrm(ks[17], (Dp, N_HEADS_MLA * MLA_V, D_MODEL), N_HEADS_MLA * MLA_V),
        "w_br_dil": nrm(ks[18], (Dp, N_HEADS_DIL * DIL_HEAD, D_MODEL), N_HEADS_DIL * DIL_HEAD),
        "w_o": nrm(ks[19], (Dp, D_MODEL, D_MODEL), D_MODEL),
        "g_ffn2": gain(ks[20], (Dp, D_MODEL)),
        "w2_gate": nrm(ks[21], (Dp, D_MODEL, D_FF), D_MODEL),
        "w2_up": nrm(ks[22], (Dp, D_MODEL, D_FF), D_MODEL),
        "w2_down": nrm(ks[23], (Dp, D_FF, D_MODEL), D_FF),
        "g_ple": gain(ks[24], (Dp, D_MODEL)),
        "w_ple_gate": nrm(ks[25], (Dp, D_MODEL, D_MODEL), D_MODEL),
        "w_ple_proj": nrm(ks[26], (Dp, PLE_DIM, D_MODEL), PLE_DIM),
    }


def reference(x, p, positions, g_ffn1, w1_gate, w1_up, w1_down, g_mix, w_in, g_cq, w_uq,
              g_ckv, w_ukv, g_q_mla, g_k_mla, g_q_dil, g_k_dil, w_br_mla, w_br_dil, w_o,
              g_ffn2, w2_gate, w2_up, w2_down, g_ple, w_ple_gate, w_ple_proj):
    B, S, _ = x.shape
    scale_mla = MLA_QK ** -0.5
    scale_dil = DIL_HEAD ** -0.5
    for i in range(DEPTH):
        x = x + 0.5 * swiglu(rmsnorm(x, g_ffn1[i]), w1_gate[i], w1_up[i], w1_down[i])

        h = rmsnorm(x, g_mix[i])
        proj = h @ w_in[i]

        cq = rmsnorm(proj[..., OFF_CQ:OFF_CKV], g_cq[i])
        q = (cq @ w_uq[i]).reshape(B, S, N_HEADS_MLA, MLA_QK)
        ckv = rmsnorm(proj[..., OFF_CKV:OFF_KR], g_ckv[i])
        kv = (ckv @ w_ukv[i]).reshape(B, S, N_HEADS_MLA, MLA_NOPE + MLA_V)
        k_nope, v = kv[..., :MLA_NOPE], kv[..., MLA_NOPE:]
        k_rope = jnp.broadcast_to(proj[:, :, None, OFF_KR:OFF_DIL], (B, S, N_HEADS_MLA, MLA_ROPE))
        k = jnp.concatenate([k_nope, k_rope], axis=-1)
        q = rmsnorm(q, g_q_mla[i])
        k = rmsnorm(k, g_k_mla[i])
        q = jnp.concatenate([q[..., :MLA_NOPE], rope(q[..., MLA_NOPE:], positions)], axis=-1) * scale_mla
        k = jnp.concatenate([k[..., :MLA_NOPE], rope(k[..., MLA_NOPE:], positions)], axis=-1)
        o_mla = causal_block_attention(q, k, v).reshape(B, S, N_HEADS_MLA * MLA_V)

        dqkv = proj[..., OFF_DIL:OFF_GATE].reshape(B, S, N_DIL_GROUPS, 3, N_HEADS_DIL, DIL_HEAD)
        outs, ms, ls = [], [], []
        for g, (win, dil) in enumerate(DIL_GROUPS):
            qg = rmsnorm(dqkv[:, :, g, 0], g_q_dil[i, g])
            kg = rmsnorm(dqkv[:, :, g, 1], g_k_dil[i, g])
            vg = dqkv[:, :, g, 2]
            qg = jnp.concatenate([rope(qg[..., :DIL_ROT], positions), qg[..., DIL_ROT:]], axis=-1) * scale_dil
            kg = jnp.concatenate([rope(kg[..., :DIL_ROT], positions), kg[..., DIL_ROT:]], axis=-1)
            o, m, l = dilated_window_attention(qg, kg, vg, dil, win // dil)
            outs.append(o)
            ms.append(m)
            ls.append(l)
        m_all = jnp.stack(ms, axis=0)
        l_all = jnp.stack(ls, axis=0)
        wgt = l_all * jnp.exp(m_all - jnp.max(m_all, axis=0, keepdims=True))
        wgt = wgt / jnp.sum(wgt, axis=0, keepdims=True)
        o_dil = jnp.einsum('gbsh,gbshc->bshc', wgt.astype(x.dtype), jnp.stack(outs, axis=0))
        o_dil = o_dil.reshape(B, S, N_HEADS_DIL * DIL_HEAD)

        gates = jax.nn.sigmoid(proj[..., OFF_GATE:].reshape(B, S, N_BRANCH, D_MODEL))
        merged = gates[:, :, 0] * (o_mla @ w_br_mla[i]) + gates[:, :, 1] * (o_dil @ w_br_dil[i])
        x = x + merged @ w_o[i]

        x = x + 0.5 * swiglu(rmsnorm(x, g_ffn2[i]), w2_gate[i], w2_up[i], w2_down[i])

        ple_gate = jax.nn.sigmoid(rmsnorm(x, g_ple[i]) @ w_ple_gate[i])
        x = x + ple_gate * (p[i] @ w_ple_proj[i])
    return x
```

```python
import functools

import jax
import jax.numpy as jnp
from jax import lax
from jax.experimental import pallas as pl
from jax.experimental.pallas import tpu as pltpu

F32 = jnp.float32
BF16 = jnp.bfloat16

EPS = 1e-6
NEG = -1e30
ROPE_THETA = 500000.0

LANES = 128
N_HEADS = 8
HEAD = 128
MLA_ROPE = 64
MLA_QK = HEAD + MLA_ROPE
MLA_PAD = 256
Q_LORA = 512
KV_LORA = 256
LAT_W = 1024
DIL_GROUPS = ((128, 1), (512, 4), (2048, 16))
N_GROUPS = len(DIL_GROUPS)
DIL_ROT = 32
BAND = 128
HW = N_HEADS * HEAD

VMEM_LIMIT = 56 * 1024 * 1024


def _cparams(*sem):
    return pltpu.CompilerParams(dimension_semantics=sem, vmem_limit_bytes=VMEM_LIMIT)


def _rmsnorm(x, g):
    ms = jnp.mean(x * x, axis=-1, keepdims=True)
    return x * lax.rsqrt(ms + EPS) * g


def _dot(a, b):
    return jnp.dot(a, b, preferred_element_type=F32)


def _dot_nt(a, b):
    return lax.dot_general(a, b, (((1,), (1,)), ((), ())), preferred_element_type=F32)


def _rope_tile(y, c, s1, s2, half):
    return y * c + pltpu.roll(y, LANES - half, 1) * s1 + pltpu.roll(y, half, 1) * s2


def _trig_body(pos_ref, inv_ref, cos_ref, sin_ref):
    ang = pos_ref[...].astype(F32) * inv_ref[...]
    cos_ref[...] = jnp.cos(ang)
    sin_ref[...] = jnp.sin(ang)


def _rope_tables(pos, half, pad_value):
    T = pos.shape[0]
    per_row = LANES // half
    rows = T // per_row
    inv = ROPE_THETA ** (-jnp.arange(half, dtype=F32) * 2.0 / (2 * half))
    pos_rep = jnp.repeat(pos, half).reshape(rows, LANES)
    inv_rep = jnp.tile(inv, per_row).reshape(1, LANES)
    tr = min(rows, 512)
    cos, sin = pl.pallas_call(
        _trig_body,
        grid=(rows // tr,),
        in_specs=[pl.BlockSpec((tr, LANES), lambda i: (i, 0)), pl.BlockSpec((1, LANES), lambda i: (0, 0))],
        out_specs=[pl.BlockSpec((tr, LANES), lambda i: (i, 0))] * 2,
        out_shape=[jax.ShapeDtypeStruct((rows, LANES), F32)] * 2,
        compiler_params=_cparams("arbitrary"),
        name="rope_trig",
    )(pos_rep, inv_rep)
    cos = cos.reshape(T, half)
    sin = sin.reshape(T, half)
    rest = LANES - 2 * half
    zeros = jnp.zeros((T, rest), F32)
    c = jnp.concatenate([cos, cos, jnp.full((T, rest), pad_value, F32)], axis=1)
    s1 = jnp.concatenate([-sin, jnp.zeros((T, half), F32), zeros], axis=1)
    s2 = jnp.concatenate([jnp.zeros((T, half), F32), sin, zeros], axis=1)
    return c, s1, s2


def _ffn_body(x_ref, g_ref, wg_ref, wu_ref, wd_ref, o_ref, h_ref):
    f = pl.program_id(1)

    @pl.when(f == 0)
    def _():
        x = x_ref[...]
        h_ref[...] = _rmsnorm(x, g_ref[...]).astype(BF16)
        o_ref[...] = x

    h = h_ref[...]
    a = _dot(h, wg_ref[...])
    u = _dot(h, wu_ref[...])
    act = (0.5 * a) * jax.nn.sigmoid(a) * u
    o_ref[...] += _dot(act.astype(BF16), wd_ref[...])


def _ffn(x, g, wg, wu, wd, *, tm, tf):
    T, D = x.shape
    F = wg.shape[1]
    return pl.pallas_call(
        _ffn_body,
        grid=(T // tm, F // tf),
        in_specs=[
            pl.BlockSpec((tm, D), lambda i, f: (i, 0)),
            pl.BlockSpec((1, D), lambda i, f: (0, 0)),
            pl.BlockSpec((D, tf), lambda i, f: (0, f)),
            pl.BlockSpec((D, tf), lambda i, f: (0, f)),
            pl.BlockSpec((tf, D), lambda i, f: (f, 0)),
        ],
        out_specs=pl.BlockSpec((tm, D), lambda i, f: (i, 0)),
        out_shape=jax.ShapeDtypeStruct((T, D), F32),
        scratch_shapes=[pltpu.VMEM((tm, D), BF16)],
        compiler_params=_cparams("parallel", "arbitrary"),
        name="ffn",
    )(x, g, wg, wu, wd)


def _proj_body(x_ref, g_ref, w_ref, gd_ref, c_ref, s1_ref, s2_ref, lat_ref, dil_ref, gate_ref, h_ref, y_ref,
               *, n_dil):
    j = pl.program_id(1)

    @pl.when(j == 0)
    def _():
        h_ref[...] = _rmsnorm(x_ref[...], g_ref[...]).astype(BF16)

    y_ref[...] = _dot(h_ref[...], w_ref[...])

    @pl.when(j == 0)
    def _():
        lat_ref[...] = y_ref[...]

    is_dil = jnp.logical_and(j >= 1, j <= n_dil)
    is_v = (j - 1) % 3 == 2

    @pl.when(jnp.logical_and(is_dil, is_v))
    def _():
        dil_ref[...] = y_ref[...].astype(BF16)

    @pl.when(jnp.logical_and(is_dil, jnp.logical_not(is_v)))
    def _():
        c = c_ref[...]
        s1 = s1_ref[...]
        s2 = s2_ref[...]
        for h in range(N_HEADS):
            sl = slice(h * HEAD, (h + 1) * HEAD)
            yn = _rmsnorm(y_ref[:, sl], gd_ref[:, sl])
            dil_ref[:, sl] = _rope_tile(yn, c, s1, s2, DIL_ROT // 2).astype(BF16)

    @pl.when(j > n_dil)
    def _():
        gate_ref[...] = jax.nn.sigmoid(y_ref[...]).astype(BF16)


def _in_proj(x, g, w, gd, c, s1, s2, *, tm):
    T, D = x.shape
    n_dil = 3 * N_GROUPS
    n_gate = (2 * D) // LAT_W
    nj = 1 + n_dil + n_gate
    assert w.shape[1] == nj * LAT_W

    def dil_idx(j):
        return jnp.clip(j - 1, 0, n_dil - 1)

    tab = pl.BlockSpec((tm, LANES), lambda i, j: (i, 0))
    return pl.pallas_call(
        functools.partial(_proj_body, n_dil=n_dil),
        grid=(T // tm, nj),
        in_specs=[
            pl.BlockSpec((tm, D), lambda i, j: (i, 0)),
            pl.BlockSpec((1, D), lambda i, j: (0, 0)),
            pl.BlockSpec((D, LAT_W), lambda i, j: (0, j)),
            pl.BlockSpec((None, 1, HW), lambda i, j: (dil_idx(j), 0, 0)),
            tab, tab, tab,
        ],
        out_specs=[
            pl.BlockSpec((tm, LAT_W), lambda i, j: (i, 0)),
            pl.BlockSpec((None, tm, HW), lambda i, j: (dil_idx(j), i, 0)),
            pl.BlockSpec((tm, LAT_W), lambda i, j: (i, jnp.clip(j - 1 - n_dil, 0, n_gate - 1))),
        ],
        out_shape=[
            jax.ShapeDtypeStruct((T, LAT_W), F32),
            jax.ShapeDtypeStruct((n_dil, T, HW), BF16),
            jax.ShapeDtypeStruct((T, 2 * D), BF16),
        ],
        scratch_shapes=[pltpu.VMEM((tm, D), BF16), pltpu.VMEM((tm, LAT_W), F32)],
        compiler_params=_cparams("parallel", "arbitrary"),
        name="in_proj",
    )(x, g, w, gd, c, s1, s2)


def _mla_prep_body(lat_ref, gcq_ref, gckv_ref, wuq_ref, wukv_ref, gqn_ref, gqr_ref, gkn_ref, gkr_ref,
                   c_ref, s1_ref, s2_ref, q_ref, k_ref, v_ref):
    half = MLA_ROPE // 2
    c = c_ref[...]
    s1 = s1_ref[...]
    s2 = s2_ref[...]
    cq = _rmsnorm(lat_ref[:, :Q_LORA], gcq_ref[...]).astype(BF16)
    ckv = _rmsnorm(lat_ref[:, Q_LORA:Q_LORA + KV_LORA], gckv_ref[...]).astype(BF16)
    qa = _dot(cq, wuq_ref[...])
    kva = _dot(ckv, wukv_ref[...])
    kr = lat_ref[:, Q_LORA + KV_LORA:Q_LORA + KV_LORA + LANES]
    kr_ss = jnp.sum(kr * kr, axis=-1, keepdims=True)
    kr_rot = _rope_tile(kr * gkr_ref[...], c, s1, s2, half)
    for h in range(N_HEADS):
        lo = h * MLA_PAD
        qn = qa[:, lo:lo + HEAD]
        qr = qa[:, lo + HEAD:lo + MLA_PAD]
        ss = jnp.sum(qn * qn, axis=-1, keepdims=True) + jnp.sum(qr * qr, axis=-1, keepdims=True)
        r = lax.rsqrt(ss * (1.0 / MLA_QK) + EPS)
        q_ref[:, lo:lo + HEAD] = (qn * r * gqn_ref[...]).astype(BF16)
        q_ref[:, lo + HEAD:lo + MLA_PAD] = (_rope_tile(qr * gqr_ref[...], c, s1, s2, half) * r).astype(BF16)
        kn = kva[:, h * HEAD:(h + 1) * HEAD]
        ssk = jnp.sum(kn * kn, axis=-1, keepdims=True) + kr_ss
        rk = lax.rsqrt(ssk * (1.0 / MLA_QK) + EPS)
        k_ref[:, lo:lo + HEAD] = (kn * rk * gkn_ref[...]).astype(BF16)
        k_ref[:, lo + HEAD:lo + MLA_PAD] = (kr_rot * rk).astype(BF16)
    v_ref[...] = kva[:, HW:].astype(BF16)


def _mla_prep(lat, gcq, gckv, wuq, wukv, gqn, gqr, gkn, gkr, c, s1, s2, *, tm):
    T = lat.shape[0]
    row = lambda w: pl.BlockSpec((tm, w), lambda i: (i, 0))
    full = lambda a: pl.BlockSpec(a.shape, lambda i: (0, 0))
    return pl.pallas_call(
        _mla_prep_body,
        grid=(T // tm,),
        in_specs=[row(LAT_W), full(gcq), full(gckv), full(wuq), full(wukv), full(gqn), full(gqr), full(gkn),
                  full(gkr), row(LANES), row(LANES), row(LANES)],
        out_specs=[row(N_HEADS * MLA_PAD), row(N_HEADS * MLA_PAD), row(HW)],
        out_shape=[
            jax.ShapeDtypeStruct((T, N_HEADS * MLA_PAD), BF16),
            jax.ShapeDtypeStruct((T, N_HEADS * MLA_PAD), BF16),
            jax.ShapeDtypeStruct((T, HW), BF16),
        ],
        compiler_params=_cparams("parallel"),
        name="mla_prep",
    )(lat, gcq, gckv, wuq, wukv, gqn, gqr, gkn, gkr, c, s1, s2)


def _mla_attn_body(q_ref, k_ref, v_ref, o_ref, *, tq):
    i = pl.program_id(2)
    q = q_ref[...]

    def step(kc, carry, masked):
        m, l, acc = carry
        start = pl.multiple_of(kc * tq, tq)
        s = _dot_nt(q, k_ref[pl.ds(start, tq), :])
        if masked:
            row = lax.broadcasted_iota(jnp.int32, (tq, tq), 0)
            col = lax.broadcasted_iota(jnp.int32, (tq, tq), 1)
            s = jnp.where(col <= row, s, NEG)
        m_new = jnp.maximum(m, jnp.max(s, axis=-1, keepdims=True))
        alpha = jnp.exp(m - m_new)
        p = jnp.exp(s - m_new)
        l = alpha * l + jnp.sum(p, axis=-1, keepdims=True)
        acc = alpha * acc + _dot(p.astype(BF16), v_ref[pl.ds(start, tq), :])
        return m_new, l, acc

    init = (jnp.full((tq, 1), NEG, F32), jnp.zeros((tq, 1), F32), jnp.zeros((tq, HEAD), F32))
    carry = lax.fori_loop(0, i, lambda kc, c: step(kc, c, False), init)
    _, l, acc = step(i, carry, True)
    o_ref[...] = (acc / l).astype(BF16)


def _mla_attn(q, k, v, *, B, S, tq):
    T = B * S
    nq = S // tq
    return pl.pallas_call(
        functools.partial(_mla_attn_body, tq=tq),
        grid=(B, N_HEADS, nq),
        in_specs=[
            pl.BlockSpec((tq, MLA_PAD), lambda b, h, i: (b * nq + i, h)),
            pl.BlockSpec((S, MLA_PAD), lambda b, h, i: (b, h)),
            pl.BlockSpec((S, HEAD), lambda b, h, i: (b, h)),
        ],
        out_specs=pl.BlockSpec((tq, HEAD), lambda b, h, i: (b * nq + i, h)),
        out_shape=jax.ShapeDtypeStruct((T, HW), BF16),
        compiler_params=_cparams("parallel", "parallel", "arbitrary"),
        name="mla_attn",
    )(q, k, v)


def _band_body(q_ref, k_ref, kh_ref, v_ref, vh_ref, o_ref, m_ref, l_ref, *, nsub):
    rb = pl.program_id(2)
    row = lax.broadcasted_iota(jnp.int32, (BAND, BAND), 0)
    col = lax.broadcasted_iota(jnp.int32, (BAND, BAND), 1)
    bias_prev = jnp.where(col >= row, 0.0, NEG).astype(F32)
    bias_cur = jnp.where(col <= row, 0.0, NEG).astype(F32)
    no_halo = jnp.where(rb == 0, NEG, 0.0).astype(F32)
    lane = lax.broadcasted_iota(jnp.int32, (BAND, LANES), 1)
    for c in range(nsub):
        rows = slice(c * BAND, (c + 1) * BAND)
        m_blk = jnp.zeros((BAND, LANES), F32)
        l_blk = jnp.ones((BAND, LANES), F32)
        for h in range(N_HEADS):
            hs = slice(h * HEAD, (h + 1) * HEAD)
            q = q_ref[rows, hs]
            if c == 0:
                k_prev, v_prev = kh_ref[:, hs], vh_ref[:, hs]
                extra = bias_prev + no_halo
            else:
                prev = slice((c - 1) * BAND, c * BAND)
                k_prev, v_prev = k_ref[prev, hs], v_ref[prev, hs]
                extra = bias_prev
            s_prev = _dot_nt(q, k_prev) + extra
            s_cur = _dot_nt(q, k_ref[rows, hs]) + bias_cur
            m = jnp.maximum(jnp.max(s_prev, axis=-1, keepdims=True), jnp.max(s_cur, axis=-1, keepdims=True))
            e_prev = jnp.exp(s_prev - m)
            e_cur = jnp.exp(s_cur - m)
            l = jnp.sum(e_prev, axis=-1, keepdims=True) + jnp.sum(e_cur, axis=-1, keepdims=True)
            o = _dot(e_prev.astype(BF16), v_prev) + _dot(e_cur.astype(BF16), v_ref[rows, hs])
            o_ref[rows, hs] = (o / l).astype(BF16)
            m_blk = jnp.where(lane == h, m, m_blk)
            l_blk = jnp.where(lane == h, l, l_blk)
        m_ref[rows, :] = m_blk
        l_ref[rows, :] = l_blk


def _band_attn(dil9, g, *, B, S):
    _, dil = DIL_GROUPS[g]
    L = S // dil
    rb_rows = min(L, 512)
    nsub = rb_rows // BAND
    view = dil9.reshape(dil9.shape[0], B, L, dil * HW)
    blk = lambda slab: pl.BlockSpec((None, None, rb_rows, HW), lambda b, r, rb: (3 * g + slab, b, rb, r))
    halo = lambda slab: pl.BlockSpec(
        (None, None, BAND, HW), lambda b, r, rb: (3 * g + slab, b, jnp.maximum(rb * nsub - 1, 0), r))
    o, m, l = pl.pallas_call(
        functools.partial(_band_body, nsub=nsub),
        grid=(B, dil, L // rb_rows),
        in_specs=[blk(0), blk(1), halo(1), blk(2), halo(2)],
        out_specs=[
            pl.BlockSpec((None, rb_rows, HW), lambda b, r, rb: (b, rb, r)),
            pl.BlockSpec((None, rb_rows, LANES), lambda b, r, rb: (b, rb, r)),
            pl.BlockSpec((None, rb_rows, LANES), lambda b, r, rb: (b, rb, r)),
        ],
        out_shape=[
            jax.ShapeDtypeStruct((B, L, dil * HW), BF16),
            jax.ShapeDtypeStruct((B, L, dil * LANES), F32),
            jax.ShapeDtypeStruct((B, L, dil * LANES), F32),
        ],
        compiler_params=_cparams("parallel", "parallel", "arbitrary"),
        name=f"band_attn_{g}",
    )(view, view, view, view, view)
    T = B * S
    return o.reshape(T, HW), m.reshape(T, LANES), l.reshape(T, LANES)


def _merge_body(x_ref, om_ref, o0_ref, o1_ref, o2_ref, m0_ref, m1_ref, m2_ref, l0_ref, l1_ref, l2_ref,
                ga_ref, gb_ref, wa_ref, wb_ref, wo_ref, out_ref, od_ref):
    f = pl.program_id(1)

    @pl.when(f == 0)
    def _():
        m0, m1, m2 = m0_ref[...], m1_ref[...], m2_ref[...]
        mx = jnp.maximum(jnp.maximum(m0, m1), m2)
        w0 = l0_ref[...] * jnp.exp(m0 - mx)
        w1 = l1_ref[...] * jnp.exp(m1 - mx)
        w2 = l2_ref[...] * jnp.exp(m2 - mx)
        tot = w0 + w1 + w2
        w0, w1, w2 = w0 / tot, w1 / tot, w2 / tot
        tm = w0.shape[0]
        for h in range(N_HEADS):
            hs = slice(h * HEAD, (h + 1) * HEAD)
            bc = lambda w: jnp.broadcast_to(w[:, h:h + 1], (tm, HEAD))
            od = (bc(w0) * o0_ref[:, hs].astype(F32) + bc(w1) * o1_ref[:, hs].astype(F32)
                  + bc(w2) * o2_ref[:, hs].astype(F32))
            od_ref[:, hs] = od.astype(BF16)
        out_ref[...] = x_ref[...]

    a = _dot(om_ref[...], wa_ref[...])
    b = _dot(od_ref[...], wb_ref[...])
    merged = ga_ref[...].astype(F32) * a + gb_ref[...].astype(F32) * b
    out_ref[...] += _dot(merged.astype(BF16), wo_ref[...])


def _merge(x, o_mla, o_dil, m_dil, l_dil, gates, wa, wb, wo, *, tm, tf):
    T, D = x.shape
    nf = D // tf
    row = lambda w: pl.BlockSpec((tm, w), lambda i, f: (i, 0))
    return pl.pallas_call(
        _merge_body,
        grid=(T // tm, nf),
        in_specs=[row(D), row(HW), row(HW), row(HW), row(HW)] + [row(LANES)] * 6 + [
            pl.BlockSpec((tm, tf), lambda i, f: (i, f)),
            pl.BlockSpec((tm, tf), lambda i, f: (i, nf + f)),
            pl.BlockSpec((HW, tf), lambda i, f: (0, f)),
            pl.BlockSpec((HW, tf), lambda i, f: (0, f)),
            pl.BlockSpec((tf, D), lambda i, f: (f, 0)),
        ],
        out_specs=row(D),
        out_shape=jax.ShapeDtypeStruct((T, D), F32),
        scratch_shapes=[pltpu.VMEM((tm, HW), BF16)],
        compiler_params=_cparams("parallel", "arbitrary"),
        name="merge",
    )(x, o_mla, *o_dil, *m_dil, *l_dil, gates, gates, wa, wb, wo)


def _ple_body(x_ref, g_ref, p_ref, wg_ref, wp_ref, o_ref):
    x = x_ref[...]
    h = _rmsnorm(x, g_ref[...]).astype(BF16)
    gate = jax.nn.sigmoid(_dot(h, wg_ref[...]))
    o_ref[...] = x + gate * _dot(p_ref[...].astype(BF16), wp_ref[...])


def _ple(x, g, p, wg, wp, *, tm):
    T, D = x.shape
    P = p.shape[1]
    return pl.pallas_call(
        _ple_body,
        grid=(T // tm,),
        in_specs=[
            pl.BlockSpec((tm, D), lambda i: (i, 0)),
            pl.BlockSpec((1, D), lambda i: (0, 0)),
            pl.BlockSpec((tm, P), lambda i: (i, 0)),
            pl.BlockSpec((D, D), lambda i: (0, 0)),
            pl.BlockSpec((P, D), lambda i: (0, 0)),
        ],
        out_specs=pl.BlockSpec((tm, D), lambda i: (i, 0)),
        out_shape=jax.ShapeDtypeStruct((T, D), F32),
        compiler_params=_cparams("parallel"),
        name="ple",
    )(x, g, p, wg, wp)


def _tile128(v):
    return jnp.tile(v, N_HEADS).reshape(1, HW)


def _layer(x, p, pos, i, g_ffn1, w1_gate, w1_up, w1_down, g_mix, w_in, g_cq, w_uq, g_ckv, w_ukv, g_q_mla, g_k_mla,
           g_q_dil, g_k_dil, w_br_mla, w_br_dil, w_o, g_ffn2, w2_gate, w2_up, w2_down, g_ple, w_ple_gate,
           w_ple_proj, tables, B, S):
    T, D = x.shape
    tm = min(T, 512)
    tf = min(w1_gate.shape[-1], 512)
    bf = lambda a: a.astype(BF16)
    row = lambda a: a.reshape(1, -1).astype(F32)
    (c_mla, s1_mla, s2_mla), (c_dil, s1_dil, s2_dil) = tables

    x = _ffn(x, row(g_ffn1[i]), bf(w1_gate[i]), bf(w1_up[i]), bf(w1_down[i]), tm=tm, tf=tf)

    off_dil = Q_LORA + KV_LORA + MLA_ROPE
    w = w_in[i]
    w_r = jnp.concatenate([w[:, :off_dil], jnp.zeros((D, LAT_W - off_dil), w.dtype), w[:, off_dil:]], axis=1)
    scale_dil = HEAD ** -0.5
    ones = jnp.ones((HEAD,), F32)
    gd = jnp.stack([_tile128(t) for g in range(N_GROUPS)
                    for t in (g_q_dil[i, g] * scale_dil, g_k_dil[i, g], ones)], axis=0)
    lat, dil9, gates = _in_proj(x, row(g_mix[i]), bf(w_r), gd, c_dil, s1_dil, s2_dil, tm=tm)

    scale_mla = MLA_QK ** -0.5
    wuq = jnp.pad(w_uq[i].reshape(Q_LORA, N_HEADS, MLA_QK), ((0, 0), (0, 0), (0, MLA_PAD - MLA_QK)))
    wuq = bf(wuq.reshape(Q_LORA, N_HEADS * MLA_PAD))
    wukv = bf(w_ukv[i].reshape(KV_LORA, N_HEADS, 2, HEAD).transpose(0, 2, 1, 3).reshape(KV_LORA, 2 * HW))
    pad_rope = lambda v: jnp.pad(v, (0, LANES - MLA_ROPE)).reshape(1, LANES)
    gq, gk = g_q_mla[i] * scale_mla, g_k_mla[i]
    q, k, v = _mla_prep(lat, row(g_cq[i]), row(g_ckv[i]), wuq, wukv, row(gq[:HEAD]), pad_rope(gq[HEAD:]),
                        row(gk[:HEAD]), pad_rope(gk[HEAD:]), c_mla, s1_mla, s2_mla, tm=tm)
    o_mla = _mla_attn(q, k, v, B=B, S=S, tq=min(S, 512))

    outs = [_band_attn(dil9, g, B=B, S=S) for g in range(N_GROUPS)]
    o_dil, m_dil, l_dil = zip(*outs)

    x = _merge(x, o_mla, o_dil, m_dil, l_dil, gates, bf(w_br_mla[i]), bf(w_br_dil[i]), bf(w_o[i]),
               tm=tm, tf=min(D, 512))
    x = _ffn(x, row(g_ffn2[i]), bf(w2_gate[i]), bf(w2_up[i]), bf(w2_down[i]), tm=tm, tf=tf)
    x = _ple(x, row(g_ple[i]), p[i].reshape(T, -1), bf(w_ple_gate[i]), bf(w_ple_proj[i]), tm=tm)
    return x


def kernel(x, p, positions, g_ffn1, w1_gate, w1_up, w1_down, g_mix, w_in, g_cq, w_uq, g_ckv, w_ukv, g_q_mla,
           g_k_mla, g_q_dil, g_k_dil, w_br_mla, w_br_dil, w_o, g_ffn2, w2_gate, w2_up, w2_down, g_ple,
           w_ple_gate, w_ple_proj):
    B, S, D = x.shape
    T = B * S
    pos = positions.reshape(T)
    tables = (_rope_tables(pos, MLA_ROPE // 2, 0.0), _rope_tables(pos, DIL_ROT // 2, 1.0))
    xt = x.reshape(T, D)
    for i in range(g_ffn1.shape[0]):
        xt = _layer(xt, p, pos, i, g_ffn1, w1_gate, w1_up, w1_down, g_mix, w_in, g_cq, w_uq, g_ckv, w_ukv,
                    g_q_mla, g_k_mla, g_q_dil, g_k_dil, w_br_mla, w_br_dil, w_o, g_ffn2, w2_gate, w2_up,
                    w2_down, g_ple, w_ple_gate, w_ple_proj, tables, B, S)
    return xt.reshape(B, S, D)
```

```python
import functools

import jax
import jax.numpy as jnp
from jax import lax
from jax.experimental import pallas as pl
from jax.experimental.pallas import tpu as pltpu

F32 = jnp.float32
BF16 = jnp.bfloat16

EPS = 1e-6
NEG = -1e30
ROPE_THETA = 500000.0

LANES = 128
N_HEADS = 8
HEAD = 128
MLA_ROPE = 64
MLA_QK = HEAD + MLA_ROPE
MLA_PAD = 256
Q_LORA = 512
KV_LORA = 256
LAT_W = 1024
DIL_GROUPS = ((128, 1), (512, 4), (2048, 16))
N_GROUPS = len(DIL_GROUPS)
DIL_ROT = 32
BAND = 128
HW = N_HEADS * HEAD

VMEM_LIMIT = 56 * 1024 * 1024


def _cparams(*sem):
    return pltpu.CompilerParams(dimension_semantics=sem, vmem_limit_bytes=VMEM_LIMIT)


def _rmsnorm(x, g):
    ms = jnp.mean(x * x, axis=-1, keepdims=True)
    return x * lax.rsqrt(ms + EPS) * g


def _dot(a, b):
    return jnp.dot(a, b, preferred_element_type=F32)


def _dot_nt(a, b):
    return lax.dot_general(a, b, (((1,), (1,)), ((), ())), preferred_element_type=F32)


def _rope_tile(y, c, s):
    return y * c + pltpu.roll(y, LANES // 2, 1) * s


def _trig_body(pos_ref, inv_ref, cos_ref, sin_ref):
    ang = pos_ref[...].astype(F32) * inv_ref[...]
    cos_ref[...] = jnp.cos(ang)
    sin_ref[...] = jnp.sin(ang)


def _rope_tables(pos, half):
    T = pos.shape[0]
    per_row = LANES // half
    rows = T // per_row
    inv = ROPE_THETA ** (-jnp.arange(half, dtype=F32) * 2.0 / (2 * half))
    pos_rep = jnp.repeat(pos, half).reshape(rows, LANES)
    inv_rep = jnp.tile(inv, per_row).reshape(1, LANES)
    tr = min(rows, 512)
    cos, sin = pl.pallas_call(
        _trig_body,
        grid=(rows // tr,),
        in_specs=[pl.BlockSpec((tr, LANES), lambda i: (i, 0)), pl.BlockSpec((1, LANES), lambda i: (0, 0))],
        out_specs=[pl.BlockSpec((tr, LANES), lambda i: (i, 0))] * 2,
        out_shape=[jax.ShapeDtypeStruct((rows, LANES), F32)] * 2,
        compiler_params=_cparams("arbitrary"),
        name="rope_trig",
    )(pos_rep, inv_rep)
    cos = cos.reshape(T, half)
    sin = sin.reshape(T, half)
    rest = LANES // 2 - half
    one = jnp.ones((T, rest), F32)
    zero = jnp.zeros((T, rest), F32)
    c = jnp.concatenate([cos, one, cos, one], axis=1)
    s = jnp.concatenate([-sin, zero, sin, zero], axis=1)
    return c, s


def _split_pair(a, start, half, axis):
    idx = list(range(LANES))
    x1 = idx[start:start + half]
    x2 = idx[start + half:start + 2 * half]
    other = [t for t in idx if t not in x1 and t not in x2]
    n = LANES // 2 - half
    perm = x1 + other[:n] + x2 + other[n:]
    return jnp.take(a, jnp.asarray(perm, jnp.int32), axis=axis)


def _ffn_body(x_ref, g_ref, wg_ref, wu_ref, wd_ref, o_ref, h_ref):
    f = pl.program_id(1)

    @pl.when(f == 0)
    def _():
        x = x_ref[...]
        h_ref[...] = _rmsnorm(x, g_ref[...]).astype(BF16)
        o_ref[...] = x

    h = h_ref[...]
    a = _dot(h, wg_ref[...])
    u = _dot(h, wu_ref[...])
    act = (0.5 * a) * jax.nn.sigmoid(a) * u
    o_ref[...] += _dot(act.astype(BF16), wd_ref[...])


def _ffn(x, g, wg, wu, wd, *, tm, tf):
    T, D = x.shape
    F = wg.shape[1]
    return pl.pallas_call(
        _ffn_body,
        grid=(T // tm, F // tf),
        in_specs=[
            pl.BlockSpec((tm, D), lambda i, f: (i, 0)),
            pl.BlockSpec((1, D), lambda i, f: (0, 0)),
            pl.BlockSpec((D, tf), lambda i, f: (0, f)),
            pl.BlockSpec((D, tf), lambda i, f: (0, f)),
            pl.BlockSpec((tf, D), lambda i, f: (f, 0)),
        ],
        out_specs=pl.BlockSpec((tm, D), lambda i, f: (i, 0)),
        out_shape=jax.ShapeDtypeStruct((T, D), F32),
        scratch_shapes=[pltpu.VMEM((tm, D), BF16)],
        compiler_params=_cparams("parallel", "arbitrary"),
        name="ffn",
    )(x, g, wg, wu, wd)


def _proj_body(x_ref, g_ref, w_ref, gd_ref, c_ref, s_ref, lat_ref, d0_ref, d1_ref, d2_ref, gate_ref,
               h_ref, y_ref, z_ref, *, n_dil):
    j = pl.program_id(1)
    tm = y_ref.shape[0]

    @pl.when(j == 0)
    def _():
        h_ref[...] = _rmsnorm(x_ref[...], g_ref[...]).astype(BF16)

    y_ref[...] = _dot(h_ref[...], w_ref[...])

    @pl.when(j == 0)
    def _():
        lat_ref[...] = y_ref[...]

    is_dil = jnp.logical_and(j >= 1, j <= n_dil)
    is_v = (j - 1) % 3 == 2

    @pl.when(jnp.logical_and(is_dil, is_v))
    def _():
        for h in range(N_HEADS):
            z_ref[h] = y_ref[:, h * HEAD:(h + 1) * HEAD]

    @pl.when(jnp.logical_and(is_dil, jnp.logical_not(is_v)))
    def _():
        c = c_ref[...]
        s = s_ref[...]
        pair = 2 * HEAD
        same_head = (lax.broadcasted_iota(jnp.int32, (pair, pair), 0) // HEAD
                     == lax.broadcasted_iota(jnp.int32, (pair, pair), 1) // HEAD)
        ones_bd = jnp.where(same_head, 1.0, 0.0).astype(BF16)
        for hp in range(N_HEADS // 2):
            sl = slice(hp * pair, (hp + 1) * pair)
            y = y_ref[:, sl]
            ss = _dot((y * y).astype(BF16), ones_bd)
            yn = y * lax.rsqrt(ss * (1.0 / HEAD) + EPS) * gd_ref[:, sl]
            for t in range(2):
                z_ref[2 * hp + t] = _rope_tile(yn[:, t * HEAD:(t + 1) * HEAD], c, s)

    for g, d_ref in enumerate((d0_ref, d1_ref, d2_ref)):
        dil = DIL_GROUPS[g][1]

        @pl.when(jnp.logical_and(j >= 1 + 3 * g, j <= 3 + 3 * g))
        def _(d_ref=d_ref, dil=dil):
            for h in range(N_HEADS):
                for r in range(dil):
                    rows = pl.ds(r, tm // dil, stride=dil) if dil > 1 else slice(None)
                    d_ref[r, :, h * HEAD:(h + 1) * HEAD] = z_ref[h, rows, :].astype(BF16)

    @pl.when(j > n_dil)
    def _():
        gate_ref[...] = jax.nn.sigmoid(y_ref[...]).astype(BF16)


def _in_proj(x, g, w, gd, c, s, *, tm, B, S):
    T, D = x.shape
    n_dil = 3 * N_GROUPS
    n_gate = (2 * D) // LAT_W
    nj = 1 + n_dil + n_gate
    assert w.shape[1] == nj * LAT_W
    per_b = S // tm

    def dil_spec(g):
        dil = DIL_GROUPS[g][1]
        return pl.BlockSpec((None, None, dil, tm // dil, HW),
                            lambda i, j: (jnp.clip(j - 1 - 3 * g, 0, 2), i // per_b, 0, i % per_b, 0))

    def dil_shape(g):
        dil = DIL_GROUPS[g][1]
        return jax.ShapeDtypeStruct((3, B, dil, S // dil, HW), BF16)

    tab = pl.BlockSpec((tm, LANES), lambda i, j: (i, 0))
    return pl.pallas_call(
        functools.partial(_proj_body, n_dil=n_dil),
        grid=(T // tm, nj),
        in_specs=[
            pl.BlockSpec((tm, D), lambda i, j: (i, 0)),
            pl.BlockSpec((1, D), lambda i, j: (0, 0)),
            pl.BlockSpec((D, LAT_W), lambda i, j: (0, j)),
            pl.BlockSpec((None, 1, HW), lambda i, j: (jnp.clip(j - 1, 0, n_dil - 1), 0, 0)),
            tab, tab,
        ],
        out_specs=[
            pl.BlockSpec((tm, LAT_W), lambda i, j: (i, 0)),
            dil_spec(0), dil_spec(1), dil_spec(2),
            pl.BlockSpec((tm, LAT_W), lambda i, j: (i, jnp.clip(j - 1 - n_dil, 0, n_gate - 1))),
        ],
        out_shape=[
            jax.ShapeDtypeStruct((T, LAT_W), F32),
            dil_shape(0), dil_shape(1), dil_shape(2),
            jax.ShapeDtypeStruct((T, 2 * D), BF16),
        ],
        scratch_shapes=[pltpu.VMEM((tm, D), BF16), pltpu.VMEM((tm, LAT_W), F32),
                        pltpu.VMEM((N_HEADS, tm, HEAD), F32)],
        compiler_params=_cparams("parallel", "arbitrary"),
        name="in_proj",
    )(x, g, w, gd, c, s)


def _mla_prep_body(lat_ref, gcq_ref, gckv_ref, wuq_ref, wukv_ref, gqn_ref, gqr_ref, gkn_ref, gkr_ref,
                   c_ref, s_ref, q_ref, k_ref, v_ref):
    c = c_ref[...]
    s = s_ref[...]
    cq = _rmsnorm(lat_ref[:, :Q_LORA], gcq_ref[...]).astype(BF16)
    ckv = _rmsnorm(lat_ref[:, Q_LORA:Q_LORA + KV_LORA], gckv_ref[...]).astype(BF16)
    qa = _dot(cq, wuq_ref[...])
    kva = _dot(ckv, wukv_ref[...])
    kr = lat_ref[:, Q_LORA + KV_LORA:Q_LORA + KV_LORA + LANES]
    kr_ss = jnp.sum(kr * kr, axis=-1, keepdims=True)
    kr_rot = _rope_tile(kr * gkr_ref[...], c, s)
    for h in range(N_HEADS):
        lo = h * MLA_PAD
        qn = qa[:, lo:lo + HEAD]
        qr = qa[:, lo + HEAD:lo + MLA_PAD]
        ss = jnp.sum(qn * qn, axis=-1, keepdims=True) + jnp.sum(qr * qr, axis=-1, keepdims=True)
        r = lax.rsqrt(ss * (1.0 / MLA_QK) + EPS)
        q_ref[:, lo:lo + HEAD] = (qn * r * gqn_ref[...]).astype(BF16)
        q_ref[:, lo + HEAD:lo + MLA_PAD] = (_rope_tile(qr * gqr_ref[...], c, s) * r).astype(BF16)
        kn = kva[:, h * HEAD:(h + 1) * HEAD]
        ssk = jnp.sum(kn * kn, axis=-1, keepdims=True) + kr_ss
        rk = lax.rsqrt(ssk * (1.0 / MLA_QK) + EPS)
        k_ref[:, lo:lo + HEAD] = (kn * rk * gkn_ref[...]).astype(BF16)
        k_ref[:, lo + HEAD:lo + MLA_PAD] = (kr_rot * rk).astype(BF16)
    v_ref[...] = kva[:, HW:].astype(BF16)


def _mla_prep(lat, gcq, gckv, wuq, wukv, gqn, gqr, gkn, gkr, c, s, *, tm):
    T = lat.shape[0]
    row = lambda w: pl.BlockSpec((tm, w), lambda i: (i, 0))
    full = lambda a: pl.BlockSpec(a.shape, lambda i: (0, 0))
    return pl.pallas_call(
        _mla_prep_body,
        grid=(T // tm,),
        in_specs=[row(LAT_W), full(gcq), full(gckv), full(wuq), full(wukv), full(gqn), full(gqr), full(gkn),
                  full(gkr), row(LANES), row(LANES)],
        out_specs=[row(N_HEADS * MLA_PAD), row(N_HEADS * MLA_PAD), row(HW)],
        out_shape=[
            jax.ShapeDtypeStruct((T, N_HEADS * MLA_PAD), BF16),
            jax.ShapeDtypeStruct((T, N_HEADS * MLA_PAD), BF16),
            jax.ShapeDtypeStruct((T, HW), BF16),
        ],
        compiler_params=_cparams("parallel"),
        name="mla_prep",
    )(lat, gcq, gckv, wuq, wukv, gqn, gqr, gkn, gkr, c, s)


def _mla_attn_body(q_ref, k_ref, v_ref, o_ref, *, tq, tk):
    i = pl.program_id(2)
    q = q_ref[...]

    def update(kstart, carry, diag_offset):
        m, l, acc = carry
        s = _dot_nt(q, k_ref[pl.ds(kstart, tk), :])
        if diag_offset is not None:
            row = lax.broadcasted_iota(jnp.int32, s.shape, 0)
            col = lax.broadcasted_iota(jnp.int32, s.shape, 1) + diag_offset
            s = jnp.where(col <= row, s, NEG)
        m_new = jnp.maximum(m, jnp.max(s, axis=-1, keepdims=True))
        alpha = jnp.exp2(m - m_new)
        p = jnp.exp2(s - m_new)
        l = alpha * l + jnp.sum(p, axis=-1, keepdims=True)
        acc = alpha * acc + _dot(p.astype(BF16), v_ref[pl.ds(kstart, tk), :])
        return m_new, l, acc

    init = (jnp.full((tq, 1), NEG, F32), jnp.zeros((tq, 1), F32), jnp.zeros((tq, HEAD), F32))
    carry = lax.fori_loop(0, i * (tq // tk), lambda kc, c: update(pl.multiple_of(kc * tk, tk), c, None), init)
    for c in range(tq // tk):
        carry = update(pl.multiple_of(i * tq + c * tk, tk), carry, c * tk)
    _, l, acc = carry
    o_ref[...] = (acc / l).astype(BF16)


def _mla_attn(q, k, v, *, B, S, tq, tk):
    T = B * S
    nq = S // tq
    return pl.pallas_call(
        functools.partial(_mla_attn_body, tq=tq, tk=tk),
        grid=(B, N_HEADS, nq),
        in_specs=[
            pl.BlockSpec((tq, MLA_PAD), lambda b, h, i: (b * nq + i, h)),
            pl.BlockSpec((S, MLA_PAD), lambda b, h, i: (b, h)),
            pl.BlockSpec((S, HEAD), lambda b, h, i: (b, h)),
        ],
        out_specs=pl.BlockSpec((tq, HEAD), lambda b, h, i: (b * nq + i, h)),
        out_shape=jax.ShapeDtypeStruct((T, HW), BF16),
        compiler_params=_cparams("parallel", "parallel", "arbitrary"),
        name="mla_attn",
    )(q, k, v)


def _band_body(q_ref, k_ref, kh_ref, v_ref, vh_ref, o_ref, m_ref, l_ref, o32_ref, *, dil, nsub):
    rb = pl.program_id(1)
    h = pl.program_id(2)
    row = lax.broadcasted_iota(jnp.int32, (BAND, BAND), 0)
    col = lax.broadcasted_iota(jnp.int32, (BAND, BAND), 1)
    bias_prev = jnp.where(col >= row, 0.0, NEG).astype(F32)
    bias_cur = jnp.where(col <= row, 0.0, NEG).astype(F32)
    no_halo = jnp.where(rb == 0, NEG, 0.0).astype(F32)
    is_head = lax.broadcasted_iota(jnp.int32, (BAND, LANES), 1) == h

    @pl.when(h == 0)
    def _():
        m_ref[...] = jnp.zeros_like(m_ref)
        l_ref[...] = jnp.ones_like(l_ref)

    for r in range(dil):
        for c in range(nsub):
            rows = slice(c * BAND, (c + 1) * BAND)
            q = q_ref[r, rows, :]
            if c == 0:
                k_prev, v_prev = kh_ref[r], vh_ref[r]
                extra = bias_prev + no_halo
            else:
                prev = slice((c - 1) * BAND, c * BAND)
                k_prev, v_prev = k_ref[r, prev, :], v_ref[r, prev, :]
                extra = bias_prev
            s_prev = _dot_nt(q, k_prev) + extra
            s_cur = _dot_nt(q, k_ref[r, rows, :]) + bias_cur
            m = jnp.maximum(jnp.max(s_prev, axis=-1, keepdims=True), jnp.max(s_cur, axis=-1, keepdims=True))
            e_prev = jnp.exp2(s_prev - m)
            e_cur = jnp.exp2(s_cur - m)
            l = jnp.sum(e_prev, axis=-1, keepdims=True) + jnp.sum(e_cur, axis=-1, keepdims=True)
            o = _dot(e_prev.astype(BF16), v_prev) + _dot(e_cur.astype(BF16), v_ref[r, rows, :])
            tok = pl.ds(c * BAND * dil + r, BAND, stride=dil) if dil > 1 else rows
            o32_ref[tok, :] = o / l
            m_ref[tok, :] = jnp.where(is_head, m, m_ref[tok, :])
            l_ref[tok, :] = jnp.where(is_head, l, l_ref[tok, :])
    o_ref[...] = o32_ref[...].astype(BF16)


def _band_attn(slabs, g, *, B, S):
    dil = DIL_GROUPS[g][1]
    L = S // dil
    chunk = min(S, 2048)
    rb_rows = chunk // dil
    nsub = rb_rows // BAND
    nrb = L // rb_rows
    blk = lambda slab: pl.BlockSpec((None, None, dil, rb_rows, HEAD), lambda b, rb, h: (slab, b, 0, rb, h))
    halo = lambda slab: pl.BlockSpec(
        (None, None, dil, BAND, HEAD), lambda b, rb, h: (slab, b, 0, jnp.maximum(rb * nsub - 1, 0), h))
    T = B * S
    return pl.pallas_call(
        functools.partial(_band_body, dil=dil, nsub=nsub),
        grid=(B, nrb, N_HEADS),
        in_specs=[blk(0), blk(1), halo(1), blk(2), halo(2)],
        out_specs=[
            pl.BlockSpec((chunk, HEAD), lambda b, rb, h: (b * nrb + rb, h)),
            pl.BlockSpec((chunk, LANES), lambda b, rb, h: (b * nrb + rb, 0)),
            pl.BlockSpec((chunk, LANES), lambda b, rb, h: (b * nrb + rb, 0)),
        ],
        out_shape=[
            jax.ShapeDtypeStruct((T, HW), BF16),
            jax.ShapeDtypeStruct((T, LANES), F32),
            jax.ShapeDtypeStruct((T, LANES), F32),
        ],
        scratch_shapes=[pltpu.VMEM((chunk, HEAD), F32)],
        compiler_params=_cparams("parallel", "parallel", "arbitrary"),
        name=f"band_attn_{g}",
    )(slabs, slabs, slabs, slabs, slabs)


def _merge_body(x_ref, om_ref, o0_ref, o1_ref, o2_ref, m0_ref, m1_ref, m2_ref, l0_ref, l1_ref, l2_ref,
                ga_ref, gb_ref, wa_ref, wb_ref, wo_ref, out_ref, od_ref):
    f = pl.program_id(1)

    @pl.when(f == 0)
    def _():
        m0, m1, m2 = m0_ref[...], m1_ref[...], m2_ref[...]
        mx = jnp.maximum(jnp.maximum(m0, m1), m2)
        w0 = l0_ref[...] * jnp.exp2(m0 - mx)
        w1 = l1_ref[...] * jnp.exp2(m1 - mx)
        w2 = l2_ref[...] * jnp.exp2(m2 - mx)
        tot = w0 + w1 + w2
        w0, w1, w2 = w0 / tot, w1 / tot, w2 / tot
        tm = w0.shape[0]
        for h in range(N_HEADS):
            hs = slice(h * HEAD, (h + 1) * HEAD)
            bc = lambda w: jnp.broadcast_to(w[:, h:h + 1], (tm, HEAD))
            od = (bc(w0) * o0_ref[:, hs].astype(F32) + bc(w1) * o1_ref[:, hs].astype(F32)
                  + bc(w2) * o2_ref[:, hs].astype(F32))
            od_ref[:, hs] = od.astype(BF16)
        out_ref[...] = x_ref[...]

    a = _dot(om_ref[...], wa_ref[...])
    b = _dot(od_ref[...], wb_ref[...])
    merged = ga_ref[...].astype(F32) * a + gb_ref[...].astype(F32) * b
    out_ref[...] += _dot(merged.astype(BF16), wo_ref[...])


def _merge(x, o_mla, o_dil, m_dil, l_dil, gates, wa, wb, wo, *, tm, tf):
    T, D = x.shape
    nf = D // tf
    row = lambda w: pl.BlockSpec((tm, w), lambda i, f: (i, 0))
    return pl.pallas_call(
        _merge_body,
        grid=(T // tm, nf),
        in_specs=[row(D), row(HW), row(HW), row(HW), row(HW)] + [row(LANES)] * 6 + [
            pl.BlockSpec((tm, tf), lambda i, f: (i, f)),
            pl.BlockSpec((tm, tf), lambda i, f: (i, nf + f)),
            pl.BlockSpec((HW, tf), lambda i, f: (0, f)),
            pl.BlockSpec((HW, tf), lambda i, f: (0, f)),
            pl.BlockSpec((tf, D), lambda i, f: (f, 0)),
        ],
        out_specs=row(D),
        out_shape=jax.ShapeDtypeStruct((T, D), F32),
        scratch_shapes=[pltpu.VMEM((tm, HW), BF16)],
        compiler_params=_cparams("parallel", "arbitrary"),
        name="merge",
    )(x, o_mla, *o_dil, *m_dil, *l_dil, gates, gates, wa, wb, wo)


def _ple_body(x_ref, g_ref, p_ref, wg_ref, wp_ref, o_ref):
    x = x_ref[...]
    h = _rmsnorm(x, g_ref[...]).astype(BF16)
    gate = jax.nn.sigmoid(_dot(h, wg_ref[...]))
    o_ref[...] = x + gate * _dot(p_ref[...].astype(BF16), wp_ref[...])


def _ple(x, g, p, wg, wp, *, tm):
    T, D = x.shape
    P = p.shape[1]
    return pl.pallas_call(
        _ple_body,
        grid=(T // tm,),
        in_specs=[
            pl.BlockSpec((tm, D), lambda i: (i, 0)),
            pl.BlockSpec((1, D), lambda i: (0, 0)),
            pl.BlockSpec((tm, P), lambda i: (i, 0)),
            pl.BlockSpec((D, D), lambda i: (0, 0)),
            pl.BlockSpec((P, D), lambda i: (0, 0)),
        ],
        out_specs=pl.BlockSpec((tm, D), lambda i: (i, 0)),
        out_shape=jax.ShapeDtypeStruct((T, D), F32),
        compiler_params=_cparams("parallel"),
        name="ple",
    )(x, g, p, wg, wp)


def _spread_rope(a):
    half = MLA_ROPE // 2
    z = jnp.zeros(a.shape[:-1] + (LANES // 2 - half,), a.dtype)
    return jnp.concatenate([a[..., :half], z, a[..., half:], z], axis=-1)


def _layer(x, p, i, g_ffn1, w1_gate, w1_up, w1_down, g_mix, w_in, g_cq, w_uq, g_ckv, w_ukv, g_q_mla, g_k_mla,
           g_q_dil, g_k_dil, w_br_mla, w_br_dil, w_o, g_ffn2, w2_gate, w2_up, w2_down, g_ple, w_ple_gate,
           w_ple_proj, tables, B, S):
    T, D = x.shape
    tm = min(T, 512)
    tf = min(w1_gate.shape[-1], 512)
    bf = lambda a: a.astype(BF16)
    row = lambda a: a.reshape(1, -1).astype(F32)
    (c_mla, s_mla), (c_dil, s_dil) = tables
    log2e = 1.4426950408889634

    x = _ffn(x, row(g_ffn1[i]), bf(w1_gate[i]), bf(w1_up[i]), bf(w1_down[i]), tm=tm, tf=tf)

    off_kr = Q_LORA + KV_LORA
    off_dil = off_kr + MLA_ROPE
    off_gate = off_dil + 3 * N_GROUPS * HW
    w = bf(w_in[i])
    wd = w[:, off_dil:off_gate].reshape(D, N_GROUPS, 3, N_HEADS, HEAD)
    wd_qk = _split_pair(wd[:, :, :2], 0, DIL_ROT // 2, axis=-1)
    wd = jnp.concatenate([wd_qk, wd[:, :, 2:]], axis=2).reshape(D, 3 * N_GROUPS * HW)
    w_r = jnp.concatenate([w[:, :off_kr], _spread_rope(w[:, off_kr:off_dil]),
                           jnp.zeros((D, LAT_W - off_kr - LANES), BF16), wd, w[:, off_gate:]], axis=1)
    scale_dil = HEAD ** -0.5 * log2e
    pair = lambda v: _split_pair(v, 0, DIL_ROT // 2, axis=-1)
    tile_heads = lambda v: jnp.tile(v, N_HEADS).reshape(1, HW)
    gd = jnp.stack([tile_heads(t) for g in range(N_GROUPS)
                    for t in (pair(g_q_dil[i, g]) * scale_dil, pair(g_k_dil[i, g]), jnp.ones((HEAD,), F32))],
                   axis=0)
    lat, d0, d1, d2, gates = _in_proj(x, row(g_mix[i]), w_r, gd, c_dil, s_dil, tm=tm, B=B, S=S)

    scale_mla = MLA_QK ** -0.5 * log2e
    wuq = w_uq[i].reshape(Q_LORA, N_HEADS, MLA_QK)
    wuq = jnp.concatenate([wuq[..., :HEAD], _spread_rope(wuq[..., HEAD:])], axis=-1)
    wuq = bf(wuq.reshape(Q_LORA, N_HEADS * MLA_PAD))
    wukv = bf(w_ukv[i].reshape(KV_LORA, N_HEADS, 2, HEAD).transpose(0, 2, 1, 3).reshape(KV_LORA, 2 * HW))
    gq, gk = g_q_mla[i] * scale_mla, g_k_mla[i]
    q, k, v = _mla_prep(lat, row(g_cq[i]), row(g_ckv[i]), wuq, wukv, row(gq[:HEAD]), row(_spread_rope(gq[HEAD:])),
                        row(gk[:HEAD]), row(_spread_rope(gk[HEAD:])), c_mla, s_mla, tm=tm)
    o_mla = _mla_attn(q, k, v, B=B, S=S, tq=min(S, 1024), tk=min(S, 512))

    outs = [_band_attn(d, g, B=B, S=S) for g, d in enumerate((d0, d1, d2))]
    o_dil, m_dil, l_dil = zip(*outs)

    x = _merge(x, o_mla, o_dil, m_dil, l_dil, gates, bf(w_br_mla[i]), bf(w_br_dil[i]), bf(w_o[i]),
               tm=tm, tf=min(D, 512))
    x = _ffn(x, row(g_ffn2[i]), bf(w2_gate[i]), bf(w2_up[i]), bf(w2_down[i]), tm=tm, tf=tf)
    x = _ple(x, row(g_ple[i]), p[i].reshape(T, -1), bf(w_ple_gate[i]), bf(w_ple_proj[i]), tm=tm)
    return x


def kernel(x, p, positions, g_ffn1, w1_gate, w1_up, w1_down, g_mix, w_in, g_cq, w_uq, g_ckv, w_ukv, g_q_mla,
           g_k_mla, g_q_dil, g_k_dil, w_br_mla, w_br_dil, w_o, g_ffn2, w2_gate, w2_up, w2_down, g_ple,
           w_ple_gate, w_ple_proj):
    B, S, D = x.shape
    T = B * S
    pos = positions.reshape(T)
    tables = (_rope_tables(pos, MLA_ROPE // 2), _rope_tables(pos, DIL_ROT // 2))
    xt = x.reshape(T, D)
    for i in range(g_ffn1.shape[0]):
        xt = _layer(xt, p, i, g_ffn1, w1_gate, w1_up, w1_down, g_mix, w_in, g_cq, w_uq, g_ckv, w_ukv,
                    g_q_mla, g_k_mla, g_q_dil, g_k_dil, w_br_mla, w_br_dil, w_o, g_ffn2, w2_gate, w2_up,
                    w2_down, g_ple, w_ple_gate, w_ple_proj, tables, B, S)
    return xt.reshape(B, S, D)
```

```python
import functools

import jax
import jax.numpy as jnp
from jax import lax
from jax.experimental import pallas as pl
from jax.experimental.pallas import tpu as pltpu

F32 = jnp.float32
BF16 = jnp.bfloat16

EPS = 1e-6
NEG = -1e30
ROPE_THETA = 500000.0

LANES = 128
N_HEADS = 8
HEAD = 128
MLA_ROPE = 64
MLA_QK = HEAD + MLA_ROPE
MLA_PAD = 256
Q_LORA = 512
KV_LORA = 256
LAT_W = 1024
DIL_GROUPS = ((128, 1), (512, 4), (2048, 16))
N_GROUPS = len(DIL_GROUPS)
DIL_ROT = 32
BAND = 128
HW = N_HEADS * HEAD

VMEM_LIMIT = 56 * 1024 * 1024


def _cparams(*sem):
    return pltpu.CompilerParams(dimension_semantics=sem, vmem_limit_bytes=VMEM_LIMIT)


def _rmsnorm(x, g):
    ms = jnp.mean(x * x, axis=-1, keepdims=True)
    return x * lax.rsqrt(ms + EPS) * g


def _dot(a, b):
    return jnp.dot(a, b, preferred_element_type=F32)


def _dot_nt(a, b):
    return lax.dot_general(a, b, (((1,), (1,)), ((), ())), preferred_element_type=F32)


def _rope_tile(y, c, s):
    return y * c + pltpu.roll(y, LANES // 2, 1) * s


def _trig_body(pos_ref, inv_ref, cos_ref, sin_ref):
    ang = pos_ref[...].astype(F32) * inv_ref[...]
    cos_ref[...] = jnp.cos(ang)
    sin_ref[...] = jnp.sin(ang)


def _rope_tables(pos, half):
    T = pos.shape[0]
    per_row = LANES // half
    rows = T // per_row
    inv = ROPE_THETA ** (-jnp.arange(half, dtype=F32) * 2.0 / (2 * half))
    pos_rep = jnp.broadcast_to(pos[:, None], (T, half)).reshape(rows, LANES)
    inv_rep = jnp.tile(inv, per_row).reshape(1, LANES)
    tr = min(rows, 512)
    cos, sin = pl.pallas_call(
        _trig_body,
        grid=(rows // tr,),
        in_specs=[pl.BlockSpec((tr, LANES), lambda i: (i, 0)), pl.BlockSpec((1, LANES), lambda i: (0, 0))],
        out_specs=[pl.BlockSpec((tr, LANES), lambda i: (i, 0))] * 2,
        out_shape=[jax.ShapeDtypeStruct((rows, LANES), F32)] * 2,
        compiler_params=_cparams("arbitrary"),
        name="rope_trig",
    )(pos_rep, inv_rep)
    cos = cos.reshape(T, half)
    sin = sin.reshape(T, half)
    rest = LANES // 2 - half
    one = jnp.ones((T, rest), F32)
    zero = jnp.zeros((T, rest), F32)
    c = jnp.concatenate([cos, one, cos, one], axis=1)
    s = jnp.concatenate([-sin, zero, sin, zero], axis=1)
    return c, s


def _split_pair(a, half):
    n = LANES // 2 - half
    return jnp.concatenate([a[..., :half], a[..., 2 * half:2 * half + n], a[..., half:2 * half],
                            a[..., 2 * half + n:]], axis=-1)


def _ffn_body(x_ref, g_ref, wg_ref, wu_ref, wd_ref, o_ref, h_ref):
    f = pl.program_id(1)

    @pl.when(f == 0)
    def _():
        x = x_ref[...]
        h_ref[...] = _rmsnorm(x, g_ref[...]).astype(BF16)
        o_ref[...] = x

    h = h_ref[...]
    a = _dot(h, wg_ref[...])
    u = _dot(h, wu_ref[...])
    act = (0.5 * a) * jax.nn.sigmoid(a) * u
    o_ref[...] += _dot(act.astype(BF16), wd_ref[...])


def _ffn(x, g, wg, wu, wd, *, tm, tf):
    T, D = x.shape
    F = wg.shape[1]
    return pl.pallas_call(
        _ffn_body,
        grid=(T // tm, F // tf),
        in_specs=[
            pl.BlockSpec((tm, D), lambda i, f: (i, 0)),
            pl.BlockSpec((1, D), lambda i, f: (0, 0)),
            pl.BlockSpec((D, tf), lambda i, f: (0, f)),
            pl.BlockSpec((D, tf), lambda i, f: (0, f)),
            pl.BlockSpec((tf, D), lambda i, f: (f, 0)),
        ],
        out_specs=pl.BlockSpec((tm, D), lambda i, f: (i, 0)),
        out_shape=jax.ShapeDtypeStruct((T, D), F32),
        scratch_shapes=[pltpu.VMEM((tm, D), BF16)],
        compiler_params=_cparams("parallel", "arbitrary"),
        name="ffn",
    )(x, g, wg, wu, wd)


def _proj_body(x_ref, g_ref, w_ref, gd_ref, c_ref, s_ref, lat_ref, d0_ref, d1_ref, d2_ref, gate_ref,
               h_ref, y_ref, z_ref, *, n_dil):
    j = pl.program_id(1)
    tm = y_ref.shape[0]

    @pl.when(j == 0)
    def _():
        h_ref[...] = _rmsnorm(x_ref[...], g_ref[...]).astype(BF16)

    y_ref[...] = _dot(h_ref[...], w_ref[...])

    @pl.when(j == 0)
    def _():
        lat_ref[...] = y_ref[...]

    is_dil = jnp.logical_and(j >= 1, j <= n_dil)
    is_v = (j - 1) % 3 == 2

    @pl.when(jnp.logical_and(is_dil, is_v))
    def _():
        for h in range(N_HEADS):
            z_ref[h] = y_ref[:, h * HEAD:(h + 1) * HEAD]

    @pl.when(jnp.logical_and(is_dil, jnp.logical_not(is_v)))
    def _():
        c = c_ref[...]
        s = s_ref[...]
        pair = 2 * HEAD
        same_head = (lax.broadcasted_iota(jnp.int32, (pair, pair), 0) // HEAD
                     == lax.broadcasted_iota(jnp.int32, (pair, pair), 1) // HEAD)
        ones_bd = jnp.where(same_head, 1.0, 0.0).astype(BF16)
        for hp in range(N_HEADS // 2):
            sl = slice(hp * pair, (hp + 1) * pair)
            y = y_ref[:, sl]
            ss = _dot((y * y).astype(BF16), ones_bd)
            yn = y * lax.rsqrt(ss * (1.0 / HEAD) + EPS) * gd_ref[:, sl]
            for t in range(2):
                z_ref[2 * hp + t] = _rope_tile(yn[:, t * HEAD:(t + 1) * HEAD], c, s)

    for g, d_ref in enumerate((d0_ref, d1_ref, d2_ref)):
        dil = DIL_GROUPS[g][1]

        @pl.when(jnp.logical_and(j >= 1 + 3 * g, j <= 3 + 3 * g))
        def _(d_ref=d_ref, dil=dil):
            for h in range(N_HEADS):
                for r in range(dil):
                    rows = pl.ds(r, tm // dil, stride=dil) if dil > 1 else slice(None)
                    d_ref[r, :, h * HEAD:(h + 1) * HEAD] = z_ref[h, rows, :].astype(BF16)

    @pl.when(j > n_dil)
    def _():
        gate_ref[...] = y_ref[...].astype(BF16)


def _in_proj(x, g, w, gd, c, s, *, tm, B, S):
    T, D = x.shape
    n_dil = 3 * N_GROUPS
    n_gate = (2 * D) // LAT_W
    nj = 1 + n_dil + n_gate
    assert w.shape[1] == nj * LAT_W
    per_b = S // tm

    def dil_spec(g):
        dil = DIL_GROUPS[g][1]
        return pl.BlockSpec((None, None, dil, tm // dil, HW),
                            lambda i, j: (jnp.clip(j - 1 - 3 * g, 0, 2), i // per_b, 0, i % per_b, 0))

    def dil_shape(g):
        dil = DIL_GROUPS[g][1]
        return jax.ShapeDtypeStruct((3, B, dil, S // dil, HW), BF16)

    tab = pl.BlockSpec((tm, LANES), lambda i, j: (i, 0))
    return pl.pallas_call(
        functools.partial(_proj_body, n_dil=n_dil),
        grid=(T // tm, nj),
        in_specs=[
            pl.BlockSpec((tm, D), lambda i, j: (i, 0)),
            pl.BlockSpec((1, D), lambda i, j: (0, 0)),
            pl.BlockSpec((D, LAT_W), lambda i, j: (0, j)),
            pl.BlockSpec((None, 1, HW), lambda i, j: (jnp.clip(j - 1, 0, n_dil - 1), 0, 0)),
            tab, tab,
        ],
        out_specs=[
            pl.BlockSpec((tm, LAT_W), lambda i, j: (i, 0)),
            dil_spec(0), dil_spec(1), dil_spec(2),
            pl.BlockSpec((tm, LAT_W), lambda i, j: (i, jnp.clip(j - 1 - n_dil, 0, n_gate - 1))),
        ],
        out_shape=[
            jax.ShapeDtypeStruct((T, LAT_W), F32),
            dil_shape(0), dil_shape(1), dil_shape(2),
            jax.ShapeDtypeStruct((T, 2 * D), BF16),
        ],
        scratch_shapes=[pltpu.VMEM((tm, D), BF16), pltpu.VMEM((tm, LAT_W), F32),
                        pltpu.VMEM((N_HEADS, tm, HEAD), F32)],
        compiler_params=_cparams("parallel", "arbitrary"),
        name="in_proj",
    )(x, g, w, gd, c, s)


def _mla_prep_body(lat_ref, gcq_ref, gckv_ref, wuq_ref, wukv_ref, gqn_ref, gqr_ref, gkn_ref, gkr_ref,
                   c_ref, s_ref, q_ref, k_ref, v_ref):
    c = c_ref[...]
    s = s_ref[...]
    cq = _rmsnorm(lat_ref[:, :Q_LORA], gcq_ref[...]).astype(BF16)
    ckv = _rmsnorm(lat_ref[:, Q_LORA:Q_LORA + KV_LORA], gckv_ref[...]).astype(BF16)
    qa = _dot(cq, wuq_ref[...])
    kva = _dot(ckv, wukv_ref[...])
    kr = lat_ref[:, Q_LORA + KV_LORA:Q_LORA + KV_LORA + LANES]
    kr_ss = jnp.sum(kr * kr, axis=-1, keepdims=True)
    kr_rot = _rope_tile(kr * gkr_ref[...], c, s)
    for h in range(N_HEADS):
        lo = h * MLA_PAD
        qn = qa[:, lo:lo + HEAD]
        qr = qa[:, lo + HEAD:lo + MLA_PAD]
        ss = jnp.sum(qn * qn, axis=-1, keepdims=True) + jnp.sum(qr * qr, axis=-1, keepdims=True)
        r = lax.rsqrt(ss * (1.0 / MLA_QK) + EPS)
        q_ref[:, lo:lo + HEAD] = (qn * r * gqn_ref[...]).astype(BF16)
        q_ref[:, lo + HEAD:lo + MLA_PAD] = (_rope_tile(qr * gqr_ref[...], c, s) * r).astype(BF16)
        kn = kva[:, h * HEAD:(h + 1) * HEAD]
        ssk = jnp.sum(kn * kn, axis=-1, keepdims=True) + kr_ss
        rk = lax.rsqrt(ssk * (1.0 / MLA_QK) + EPS)
        k_ref[:, lo:lo + HEAD] = (kn * rk * gkn_ref[...]).astype(BF16)
        k_ref[:, lo + HEAD:lo + MLA_PAD] = (kr_rot * rk).astype(BF16)
    v_ref[...] = kva[:, HW:].astype(BF16)


def _mla_prep(lat, gcq, gckv, wuq, wukv, gqn, gqr, gkn, gkr, c, s, *, tm):
    T = lat.shape[0]
    row = lambda w: pl.BlockSpec((tm, w), lambda i: (i, 0))
    full = lambda a: pl.BlockSpec(a.shape, lambda i: (0, 0))
    return pl.pallas_call(
        _mla_prep_body,
        grid=(T // tm,),
        in_specs=[row(LAT_W), full(gcq), full(gckv), full(wuq), full(wukv), full(gqn), full(gqr), full(gkn),
                  full(gkr), row(LANES), row(LANES)],
        out_specs=[row(N_HEADS * MLA_PAD), row(N_HEADS * MLA_PAD), row(HW)],
        out_shape=[
            jax.ShapeDtypeStruct((T, N_HEADS * MLA_PAD), BF16),
            jax.ShapeDtypeStruct((T, N_HEADS * MLA_PAD), BF16),
            jax.ShapeDtypeStruct((T, HW), BF16),
        ],
        compiler_params=_cparams("parallel"),
        name="mla_prep",
    )(lat, gcq, gckv, wuq, wukv, gqn, gqr, gkn, gkr, c, s)


def _mla_attn_body(q_ref, k_ref, v_ref, o_ref, *, tq, tk):
    i = pl.program_id(2)
    q = q_ref[...]

    def update(kstart, carry, diag_offset):
        m, l, acc = carry
        s = _dot_nt(q, k_ref[pl.ds(kstart, tk), :])
        if diag_offset is not None:
            row = lax.broadcasted_iota(jnp.int32, s.shape, 0)
            col = lax.broadcasted_iota(jnp.int32, s.shape, 1) + diag_offset
            s = jnp.where(col <= row, s, NEG)
        m_new = jnp.maximum(m, jnp.max(s, axis=-1, keepdims=True))
        alpha = jnp.exp2(m - m_new)
        p = jnp.exp2(s - m_new)
        l = alpha * l + jnp.sum(p, axis=-1, keepdims=True)
        acc = alpha * acc + _dot(p.astype(BF16), v_ref[pl.ds(kstart, tk), :])
        return m_new, l, acc

    init = (jnp.full((tq, 1), NEG, F32), jnp.zeros((tq, 1), F32), jnp.zeros((tq, HEAD), F32))
    carry = lax.fori_loop(0, i * (tq // tk), lambda kc, c: update(pl.multiple_of(kc * tk, tk), c, None), init)
    for c in range(tq // tk):
        carry = update(pl.multiple_of(i * tq + c * tk, tk), carry, c * tk)
    _, l, acc = carry
    o_ref[...] = (acc / l).astype(BF16)


def _mla_attn(q, k, v, *, B, S, tq, tk):
    T = B * S
    nq = S // tq
    return pl.pallas_call(
        functools.partial(_mla_attn_body, tq=tq, tk=tk),
        grid=(B, N_HEADS, nq),
        in_specs=[
            pl.BlockSpec((tq, MLA_PAD), lambda b, h, i: (b * nq + i, h)),
            pl.BlockSpec((S, MLA_PAD), lambda b, h, i: (b, h)),
            pl.BlockSpec((S, HEAD), lambda b, h, i: (b, h)),
        ],
        out_specs=pl.BlockSpec((tq, HEAD), lambda b, h, i: (b * nq + i, h)),
        out_shape=jax.ShapeDtypeStruct((T, HW), BF16),
        compiler_params=_cparams("parallel", "parallel", "arbitrary"),
        name="mla_attn",
    )(q, k, v)


def _band_body(q_ref, k_ref, kh_ref, v_ref, vh_ref, o_ref, st_ref, o32_ref, *, dil, nsub):
    rb = pl.program_id(1)
    h = pl.program_id(2)
    row = lax.broadcasted_iota(jnp.int32, (BAND, BAND), 0)
    col = lax.broadcasted_iota(jnp.int32, (BAND, BAND), 1)
    bias_prev = jnp.where(col >= row, 0.0, NEG).astype(F32)
    bias_cur = jnp.where(col <= row, 0.0, NEG).astype(F32)
    lane = lax.broadcasted_iota(jnp.int32, (BAND, LANES), 1)
    is_m = lane == h
    is_l = lane == h + N_HEADS

    @pl.when(h == 0)
    def _():
        st_ref[...] = jnp.ones_like(st_ref)

    nb = dil * nsub

    def blocks(ref):
        return ref[...].reshape(nb, BAND, HEAD)

    def prev_blocks(ref, halo_ref):
        if nsub == 1:
            return halo_ref[...]
        shifted = jnp.concatenate([halo_ref[...], ref[:, :(nsub - 1) * BAND, :]], axis=1)
        return shifted.reshape(nb, BAND, HEAD)

    bdot_nt = lambda a, b: lax.dot_general(a, b, (((2,), (2,)), ((0,), (0,))), preferred_element_type=F32)
    bdot = lambda a, b: lax.dot_general(a, b, (((2,), (1,)), ((0,), (0,))), preferred_element_type=F32)
    q = blocks(q_ref)
    blk = lax.broadcasted_iota(jnp.int32, (nb, BAND, BAND), 0)
    first = jnp.logical_and(blk % nsub == 0, rb == 0)
    s_prev = bdot_nt(q, prev_blocks(k_ref, kh_ref)) + jnp.where(first, NEG, bias_prev[None])
    s_cur = bdot_nt(q, blocks(k_ref)) + bias_cur[None]
    m = jnp.max(jnp.maximum(s_prev, s_cur), axis=-1, keepdims=True)
    e_prev = jnp.exp2(s_prev - m)
    e_cur = jnp.exp2(s_cur - m)
    l = jnp.sum(e_prev + e_cur, axis=-1, keepdims=True)
    o = bdot(e_prev.astype(BF16), prev_blocks(v_ref, vh_ref)) + bdot(e_cur.astype(BF16), blocks(v_ref))
    o = o / l
    if dil == 1:
        o_ref[...] = o.reshape(nb * BAND, HEAD).astype(BF16)
    for b in range(nb):
        r, c = divmod(b, nsub)
        tok = pl.ds(c * BAND * dil + r, BAND, stride=dil) if dil > 1 else slice(c * BAND, (c + 1) * BAND)
        if dil > 1:
            o32_ref[tok, :] = o[b]
        st_ref[tok, :] = jnp.where(is_m, m[b], jnp.where(is_l, l[b], st_ref[tok, :]))
    if dil > 1:
        o_ref[...] = o32_ref[...].astype(BF16)


def _band_attn(slabs, g, *, B, S):
    dil = DIL_GROUPS[g][1]
    L = S // dil
    chunk = min(S, 2048)
    rb_rows = chunk // dil
    nsub = rb_rows // BAND
    nrb = L // rb_rows
    blk = lambda slab: pl.BlockSpec((None, None, dil, rb_rows, HEAD), lambda b, rb, h: (slab, b, 0, rb, h))
    halo = lambda slab: pl.BlockSpec(
        (None, None, dil, BAND, HEAD), lambda b, rb, h: (slab, b, 0, jnp.maximum(rb * nsub - 1, 0), h))
    T = B * S
    return pl.pallas_call(
        functools.partial(_band_body, dil=dil, nsub=nsub),
        grid=(B, nrb, N_HEADS),
        in_specs=[blk(0), blk(1), halo(1), blk(2), halo(2)],
        out_specs=[
            pl.BlockSpec((chunk, HEAD), lambda b, rb, h: (b * nrb + rb, h)),
            pl.BlockSpec((chunk, LANES), lambda b, rb, h: (b * nrb + rb, 0)),
        ],
        out_shape=[
            jax.ShapeDtypeStruct((T, HW), BF16),
            jax.ShapeDtypeStruct((T, LANES), F32),
        ],
        scratch_shapes=[pltpu.VMEM((chunk, HEAD), F32)],
        compiler_params=_cparams("parallel", "parallel", "arbitrary"),
        name=f"band_attn_{g}",
    )(slabs, slabs, slabs, slabs, slabs)


def _merge_body(x_ref, om_ref, o0_ref, o1_ref, o2_ref, st0_ref, st1_ref, st2_ref,
                ga_ref, gb_ref, wa_ref, wb_ref, wo_ref, out_ref, od_ref):
    f = pl.program_id(1)

    @pl.when(f == 0)
    def _():
        m0, m1, m2 = st0_ref[...], st1_ref[...], st2_ref[...]
        l0, l1, l2 = (pltpu.roll(st, LANES - N_HEADS, 1) for st in (m0, m1, m2))
        mx = jnp.maximum(jnp.maximum(m0, m1), m2)
        w0 = l0 * jnp.exp2(m0 - mx)
        w1 = l1 * jnp.exp2(m1 - mx)
        w2 = l2 * jnp.exp2(m2 - mx)
        tot = w0 + w1 + w2
        w0, w1, w2 = w0 / tot, w1 / tot, w2 / tot
        tm = w0.shape[0]
        for h in range(N_HEADS):
            hs = slice(h * HEAD, (h + 1) * HEAD)
            bc = lambda w: jnp.broadcast_to(w[:, h:h + 1], (tm, HEAD))
            od = (bc(w0) * o0_ref[:, hs].astype(F32) + bc(w1) * o1_ref[:, hs].astype(F32)
                  + bc(w2) * o2_ref[:, hs].astype(F32))
            od_ref[:, hs] = od.astype(BF16)
        out_ref[...] = x_ref[...]

    a = _dot(om_ref[...], wa_ref[...])
    b = _dot(od_ref[...], wb_ref[...])
    merged = jax.nn.sigmoid(ga_ref[...].astype(F32)) * a + jax.nn.sigmoid(gb_ref[...].astype(F32)) * b
    out_ref[...] += _dot(merged.astype(BF16), wo_ref[...])


def _merge(x, o_mla, o_dil, st_dil, gates, wa, wb, wo, *, tm, tf):
    T, D = x.shape
    nf = D // tf
    row = lambda w: pl.BlockSpec((tm, w), lambda i, f: (i, 0))
    return pl.pallas_call(
        _merge_body,
        grid=(T // tm, nf),
        in_specs=[row(D), row(HW), row(HW), row(HW), row(HW)] + [row(LANES)] * 3 + [
            pl.BlockSpec((tm, tf), lambda i, f: (i, f)),
            pl.BlockSpec((tm, tf), lambda i, f: (i, nf + f)),
            pl.BlockSpec((HW, tf), lambda i, f: (0, f)),
            pl.BlockSpec((HW, tf), lambda i, f: (0, f)),
            pl.BlockSpec((tf, D), lambda i, f: (f, 0)),
        ],
        out_specs=row(D),
        out_shape=jax.ShapeDtypeStruct((T, D), F32),
        scratch_shapes=[pltpu.VMEM((tm, HW), BF16)],
        compiler_params=_cparams("parallel", "arbitrary"),
        name="merge",
    )(x, o_mla, *o_dil, *st_dil, gates, gates, wa, wb, wo)


def _ple_body(x_ref, g_ref, p_ref, wg_ref, wp_ref, o_ref):
    x = x_ref[...]
    h = _rmsnorm(x, g_ref[...]).astype(BF16)
    gate = jax.nn.sigmoid(_dot(h, wg_ref[...]))
    o_ref[...] = x + gate * _dot(p_ref[...].astype(BF16), wp_ref[...])


def _ple(x, g, p, wg, wp, *, tm):
    T, D = x.shape
    P = p.shape[1]
    return pl.pallas_call(
        _ple_body,
        grid=(T // tm,),
        in_specs=[
            pl.BlockSpec((tm, D), lambda i: (i, 0)),
            pl.BlockSpec((1, D), lambda i: (0, 0)),
            pl.BlockSpec((tm, P), lambda i: (i, 0)),
            pl.BlockSpec((D, D), lambda i: (0, 0)),
            pl.BlockSpec((P, D), lambda i: (0, 0)),
        ],
        out_specs=pl.BlockSpec((tm, D), lambda i: (i, 0)),
        out_shape=jax.ShapeDtypeStruct((T, D), F32),
        compiler_params=_cparams("parallel"),
        name="ple",
    )(x, g, p, wg, wp)


def _spread_rope(a):
    half = MLA_ROPE // 2
    z = jnp.zeros(a.shape[:-1] + (LANES // 2 - half,), a.dtype)
    return jnp.concatenate([a[..., :half], z, a[..., half:], z], axis=-1)


def _layer(x, p, i, g_ffn1, w1_gate, w1_up, w1_down, g_mix, w_in, g_cq, w_uq, g_ckv, w_ukv, g_q_mla, g_k_mla,
           g_q_dil, g_k_dil, w_br_mla, w_br_dil, w_o, g_ffn2, w2_gate, w2_up, w2_down, g_ple, w_ple_gate,
           w_ple_proj, tables, B, S):
    T, D = x.shape
    tm = min(T, 512)
    tf = min(w1_gate.shape[-1], 512)
    bf = lambda a: a.astype(BF16)
    row = lambda a: a.reshape(1, -1).astype(F32)
    (c_mla, s_mla), (c_dil, s_dil) = tables
    log2e = 1.4426950408889634

    x = _ffn(x, row(g_ffn1[i]), bf(w1_gate[i]), bf(w1_up[i]), bf(w1_down[i]), tm=tm, tf=tf)

    off_kr = Q_LORA + KV_LORA
    off_dil = off_kr + MLA_ROPE
    off_gate = off_dil + 3 * N_GROUPS * HW
    w = bf(w_in[i])
    wd = w[:, off_dil:off_gate].reshape(D, N_GROUPS, 3, N_HEADS, HEAD)
    wd_qk = _split_pair(wd[:, :, :2], DIL_ROT // 2)
    wd = jnp.concatenate([wd_qk, wd[:, :, 2:]], axis=2).reshape(D, 3 * N_GROUPS * HW)
    w_r = jnp.concatenate([w[:, :off_kr], _spread_rope(w[:, off_kr:off_dil]),
                           jnp.zeros((D, LAT_W - off_kr - LANES), BF16), wd, w[:, off_gate:]], axis=1)
    scale_dil = HEAD ** -0.5 * log2e
    pair = lambda v: _split_pair(v, DIL_ROT // 2)
    tile_heads = lambda v: jnp.tile(v, N_HEADS).reshape(1, HW)
    gd = jnp.stack([tile_heads(t) for g in range(N_GROUPS)
                    for t in (pair(g_q_dil[i, g]) * scale_dil, pair(g_k_dil[i, g]), jnp.ones((HEAD,), F32))],
                   axis=0)
    lat, d0, d1, d2, gates = _in_proj(x, row(g_mix[i]), w_r, gd, c_dil, s_dil, tm=tm, B=B, S=S)

    scale_mla = MLA_QK ** -0.5 * log2e
    wuq = w_uq[i].reshape(Q_LORA, N_HEADS, MLA_QK)
    wuq = jnp.concatenate([wuq[..., :HEAD], _spread_rope(wuq[..., HEAD:])], axis=-1)
    wuq = bf(wuq.reshape(Q_LORA, N_HEADS * MLA_PAD))
    wukv = bf(w_ukv[i].reshape(KV_LORA, N_HEADS, 2, HEAD).transpose(0, 2, 1, 3).reshape(KV_LORA, 2 * HW))
    gq, gk = g_q_mla[i] * scale_mla, g_k_mla[i]
    q, k, v = _mla_prep(lat, row(g_cq[i]), row(g_ckv[i]), wuq, wukv, row(gq[:HEAD]), row(_spread_rope(gq[HEAD:])),
                        row(gk[:HEAD]), row(_spread_rope(gk[HEAD:])), c_mla, s_mla, tm=tm)
    o_mla = _mla_attn(q, k, v, B=B, S=S, tq=min(S, 1024), tk=min(S, 512))

    outs = [_band_attn(d, g, B=B, S=S) for g, d in enumerate((d0, d1, d2))]
    o_dil, st_dil = zip(*outs)

    x = _merge(x, o_mla, o_dil, st_dil, gates, bf(w_br_mla[i]), bf(w_br_dil[i]), bf(w_o[i]),
               tm=tm, tf=min(D, 512))
    x = _ffn(x, row(g_ffn2[i]), bf(w2_gate[i]), bf(w2_up[i]), bf(w2_down[i]), tm=tm, tf=tf)
    x = _ple(x, row(g_ple[i]), p[i].reshape(T, -1), bf(w_ple_gate[i]), bf(w_ple_proj[i]), tm=tm)
    return x


def kernel(x, p, positions, g_ffn1, w1_gate, w1_up, w1_down, g_mix, w_in, g_cq, w_uq, g_ckv, w_ukv, g_q_mla,
           g_k_mla, g_q_dil, g_k_dil, w_br_mla, w_br_dil, w_o, g_ffn2, w2_gate, w2_up, w2_down, g_ple,
           w_ple_gate, w_ple_proj):
    B, S, D = x.shape
    T = B * S
    pos = positions.reshape(T)
    tables = (_rope_tables(pos, MLA_ROPE // 2), _rope_tables(pos, DIL_ROT // 2))
    xt = x.reshape(T, D)
    for i in range(g_ffn1.shape[0]):
        xt = _layer(xt, p, i, g_ffn1, w1_gate, w1_up, w1_down, g_mix, w_in, g_cq, w_uq, g_ckv, w_ukv,
                    g_q_mla, g_k_mla, g_q_dil, g_k_dil, w_br_mla, w_br_dil, w_o, g_ffn2, w2_gate, w2_up,
                    w2_down, g_ple, w_ple_gate, w_ple_proj, tables, B, S)
    return xt.reshape(B, S, D)
```

```python
import functools

import jax
import jax.numpy as jnp
from jax import lax
from jax.experimental import pallas as pl
from jax.experimental.pallas import tpu as pltpu

F32 = jnp.float32
BF16 = jnp.bfloat16

EPS = 1e-6
NEG = -1e30
ROPE_THETA = 500000.0

LANES = 128
N_HEADS = 8
HEAD = 128
MLA_ROPE = 64
MLA_QK = HEAD + MLA_ROPE
MLA_PAD = 256
Q_LORA = 512
KV_LORA = 256
LAT_W = 1024
DIL_GROUPS = ((128, 1), (512, 4), (2048, 16))
N_GROUPS = len(DIL_GROUPS)
DIL_ROT = 32
BAND = 128
HW = N_HEADS * HEAD

VMEM_LIMIT = 56 * 1024 * 1024


def _cparams(*sem):
    return pltpu.CompilerParams(dimension_semantics=sem, vmem_limit_bytes=VMEM_LIMIT)


def _rmsnorm(x, g):
    ms = jnp.mean(x * x, axis=-1, keepdims=True)
    return x * lax.rsqrt(ms + EPS) * g


def _dot(a, b):
    return jnp.dot(a, b, preferred_element_type=F32)


def _dot_nt(a, b):
    return lax.dot_general(a, b, (((1,), (1,)), ((), ())), preferred_element_type=F32)


def _rope_tile(y, c, s):
    return y * c + pltpu.roll(y, LANES // 2, 1) * s


def _trig_body(pos_ref, inv_ref, cos_ref, sin_ref):
    ang = pos_ref[...].astype(F32) * inv_ref[...]
    cos_ref[...] = jnp.cos(ang)
    sin_ref[...] = jnp.sin(ang)


def _rope_tables(pos, half):
    T = pos.shape[0]
    per_row = LANES // half
    rows = T // per_row
    inv = ROPE_THETA ** (-jnp.arange(half, dtype=F32) * 2.0 / (2 * half))
    pos_rep = jnp.broadcast_to(pos[:, None], (T, half)).reshape(rows, LANES)
    inv_rep = jnp.tile(inv, per_row).reshape(1, LANES)
    tr = min(rows, 512)
    cos, sin = pl.pallas_call(
        _trig_body,
        grid=(rows // tr,),
        in_specs=[pl.BlockSpec((tr, LANES), lambda i: (i, 0)), pl.BlockSpec((1, LANES), lambda i: (0, 0))],
        out_specs=[pl.BlockSpec((tr, LANES), lambda i: (i, 0))] * 2,
        out_shape=[jax.ShapeDtypeStruct((rows, LANES), F32)] * 2,
        compiler_params=_cparams("arbitrary"),
        name="rope_trig",
    )(pos_rep, inv_rep)
    cos = cos.reshape(T, half)
    sin = sin.reshape(T, half)
    rest = LANES // 2 - half
    one = jnp.ones((T, rest), F32)
    zero = jnp.zeros((T, rest), F32)
    c = jnp.concatenate([cos, one, cos, one], axis=1)
    s = jnp.concatenate([-sin, zero, sin, zero], axis=1)
    return c, s


def _split_pair(a, half):
    n = LANES // 2 - half
    return jnp.concatenate([a[..., :half], a[..., 2 * half:2 * half + n], a[..., half:2 * half],
                            a[..., 2 * half + n:]], axis=-1)


def _ffn_body(x_ref, g_ref, wg_ref, wu_ref, wd_ref, o_ref, h_ref):
    f = pl.program_id(1)

    @pl.when(f == 0)
    def _():
        x = x_ref[...]
        h_ref[...] = _rmsnorm(x, g_ref[...]).astype(BF16)
        o_ref[...] = x

    h = h_ref[...]
    a = _dot(h, wg_ref[...])
    u = _dot(h, wu_ref[...])
    act = (0.5 * a) * jax.nn.sigmoid(a) * u
    o_ref[...] += _dot(act.astype(BF16), wd_ref[...])


def _ffn(x, g, wg, wu, wd, *, tm, tf):
    T, D = x.shape
    F = wg.shape[1]
    return pl.pallas_call(
        _ffn_body,
        grid=(T // tm, F // tf),
        in_specs=[
            pl.BlockSpec((tm, D), lambda i, f: (i, 0)),
            pl.BlockSpec((1, D), lambda i, f: (0, 0)),
            pl.BlockSpec((D, tf), lambda i, f: (0, f)),
            pl.BlockSpec((D, tf), lambda i, f: (0, f)),
            pl.BlockSpec((tf, D), lambda i, f: (f, 0)),
        ],
        out_specs=pl.BlockSpec((tm, D), lambda i, f: (i, 0)),
        out_shape=jax.ShapeDtypeStruct((T, D), F32),
        scratch_shapes=[pltpu.VMEM((tm, D), BF16)],
        compiler_params=_cparams("parallel", "arbitrary"),
        name="ffn",
    )(x, g, wg, wu, wd)


def _proj_body(x_ref, g_ref, w_ref, gd_ref, c_ref, s_ref, lat_ref, d0_ref, d1_ref, d2_ref, gate_ref,
               h_ref, ya_ref, yb_ref, z_ref, *, n_dil, nj):
    j = pl.program_id(1)
    tm = h_ref.shape[0]
    y_refs = (ya_ref, yb_ref)

    def finish_v(y_ref):
        for h in range(N_HEADS):
            z_ref[h] = y_ref[:, h * HEAD:(h + 1) * HEAD]

    def finish_qk(y_ref):
        c = c_ref[...]
        s = s_ref[...]
        pair = 2 * HEAD
        same_head = (lax.broadcasted_iota(jnp.int32, (pair, pair), 0) // HEAD
                     == lax.broadcasted_iota(jnp.int32, (pair, pair), 1) // HEAD)
        ones_bd = jnp.where(same_head, 1.0, 0.0).astype(BF16)
        for hp in range(N_HEADS // 2):
            sl = slice(hp * pair, (hp + 1) * pair)
            y = y_ref[:, sl]
            ss = _dot((y * y).astype(BF16), ones_bd)
            yn = y * lax.rsqrt(ss * (1.0 / HEAD) + EPS) * gd_ref[:, sl]
            for t in range(2):
                z_ref[2 * hp + t] = _rope_tile(yn[:, t * HEAD:(t + 1) * HEAD], c, s)

    def scatter(d_ref, dil):
        for h in range(N_HEADS):
            for r in range(dil):
                rows = pl.ds(r, tm // dil, stride=dil) if dil > 1 else slice(None)
                d_ref[r, :, h * HEAD:(h + 1) * HEAD] = z_ref[h, rows, :].astype(BF16)

    for step in range(nj + 1):
        cur, prev = y_refs[step % 2], y_refs[(step + 1) % 2]

        @pl.when(j == step)
        def _(step=step, cur=cur, prev=prev):
            if step == 0:
                h_ref[...] = _rmsnorm(x_ref[...], g_ref[...]).astype(BF16)
            if step < nj:
                cur[...] = _dot(h_ref[...], w_ref[...])
            block = step - 1
            if block == 0:
                lat_ref[...] = prev[...]
            elif 1 <= block <= n_dil:
                g, kind = divmod(block - 1, 3)
                if kind < 2:
                    finish_qk(prev)
                else:
                    finish_v(prev)
                scatter((d0_ref, d1_ref, d2_ref)[g], DIL_GROUPS[g][1])
            elif block > n_dil:
                gate_ref[...] = prev[...].astype(BF16)


def _in_proj(x, g, w, gd, c, s, *, tm, B, S):
    T, D = x.shape
    n_dil = 3 * N_GROUPS
    n_gate = (2 * D) // LAT_W
    nj = 1 + n_dil + n_gate
    assert w.shape[1] == nj * LAT_W
    per_b = S // tm

    def dil_spec(g):
        dil = DIL_GROUPS[g][1]
        return pl.BlockSpec((None, None, dil, tm // dil, HW),
                            lambda i, j: (jnp.clip(j - 2 - 3 * g, 0, 2), i // per_b, 0, i % per_b, 0))

    def dil_shape(g):
        dil = DIL_GROUPS[g][1]
        return jax.ShapeDtypeStruct((3, B, dil, S // dil, HW), BF16)

    tab = pl.BlockSpec((tm, LANES), lambda i, j: (i, 0))
    return pl.pallas_call(
        functools.partial(_proj_body, n_dil=n_dil, nj=nj),
        grid=(T // tm, nj + 1),
        in_specs=[
            pl.BlockSpec((tm, D), lambda i, j: (i, 0)),
            pl.BlockSpec((1, D), lambda i, j: (0, 0)),
            pl.BlockSpec((D, LAT_W), lambda i, j: (0, jnp.minimum(j, nj - 1))),
            pl.BlockSpec((None, 1, HW), lambda i, j: (jnp.clip(j - 2, 0, n_dil - 1), 0, 0)),
            tab, tab,
        ],
        out_specs=[
            pl.BlockSpec((tm, LAT_W), lambda i, j: (i, 0)),
            dil_spec(0), dil_spec(1), dil_spec(2),
            pl.BlockSpec((tm, LAT_W), lambda i, j: (i, jnp.clip(j - 2 - n_dil, 0, n_gate - 1))),
        ],
        out_shape=[
            jax.ShapeDtypeStruct((T, LAT_W), F32),
            dil_shape(0), dil_shape(1), dil_shape(2),
            jax.ShapeDtypeStruct((T, 2 * D), BF16),
        ],
        scratch_shapes=[pltpu.VMEM((tm, D), BF16), pltpu.VMEM((tm, LAT_W), F32), pltpu.VMEM((tm, LAT_W), F32),
                        pltpu.VMEM((N_HEADS, tm, HEAD), F32)],
        compiler_params=_cparams("parallel", "arbitrary"),
        name="in_proj",
    )(x, g, w, gd, c, s)


def _mla_prep_body(lat_ref, gcq_ref, gckv_ref, wuq_ref, wukv_ref, gqn_ref, gqr_ref, gkn_ref, gkr_ref,
                   c_ref, s_ref, q_ref, k_ref, v_ref):
    c = c_ref[...]
    s = s_ref[...]
    cq = _rmsnorm(lat_ref[:, :Q_LORA], gcq_ref[...]).astype(BF16)
    ckv = _rmsnorm(lat_ref[:, Q_LORA:Q_LORA + KV_LORA], gckv_ref[...]).astype(BF16)
    qa = _dot(cq, wuq_ref[...])
    kva = _dot(ckv, wukv_ref[...])
    kr = lat_ref[:, Q_LORA + KV_LORA:Q_LORA + KV_LORA + LANES]
    kr_ss = jnp.sum(kr * kr, axis=-1, keepdims=True)
    kr_rot = _rope_tile(kr * gkr_ref[...], c, s)
    for h in range(N_HEADS):
        lo = h * MLA_PAD
        qn = qa[:, lo:lo + HEAD]
        qr = qa[:, lo + HEAD:lo + MLA_PAD]
        ss = jnp.sum(qn * qn, axis=-1, keepdims=True) + jnp.sum(qr * qr, axis=-1, keepdims=True)
        r = lax.rsqrt(ss * (1.0 / MLA_QK) + EPS)
        q_ref[:, lo:lo + HEAD] = (qn * r * gqn_ref[...]).astype(BF16)
        q_ref[:, lo + HEAD:lo + MLA_PAD] = (_rope_tile(qr * gqr_ref[...], c, s) * r).astype(BF16)
        kn = kva[:, h * HEAD:(h + 1) * HEAD]
        ssk = jnp.sum(kn * kn, axis=-1, keepdims=True) + kr_ss
        rk = lax.rsqrt(ssk * (1.0 / MLA_QK) + EPS)
        k_ref[:, lo:lo + HEAD] = (kn * rk * gkn_ref[...]).astype(BF16)
        k_ref[:, lo + HEAD:lo + MLA_PAD] = (kr_rot * rk).astype(BF16)
    v_ref[...] = kva[:, HW:].astype(BF16)


def _mla_prep(lat, gcq, gckv, wuq, wukv, gqn, gqr, gkn, gkr, c, s, *, tm):
    T = lat.shape[0]
    row = lambda w: pl.BlockSpec((tm, w), lambda i: (i, 0))
    full = lambda a: pl.BlockSpec(a.shape, lambda i: (0, 0))
    return pl.pallas_call(
        _mla_prep_body,
        grid=(T // tm,),
        in_specs=[row(LAT_W), full(gcq), full(gckv), full(wuq), full(wukv), full(gqn), full(gqr), full(gkn),
                  full(gkr), row(LANES), row(LANES)],
        out_specs=[row(N_HEADS * MLA_PAD), row(N_HEADS * MLA_PAD), row(HW)],
        out_shape=[
            jax.ShapeDtypeStruct((T, N_HEADS * MLA_PAD), BF16),
            jax.ShapeDtypeStruct((T, N_HEADS * MLA_PAD), BF16),
            jax.ShapeDtypeStruct((T, HW), BF16),
        ],
        compiler_params=_cparams("parallel"),
        name="mla_prep",
    )(lat, gcq, gckv, wuq, wukv, gqn, gqr, gkn, gkr, c, s)


def _mla_attn_body(q_ref, k_ref, v_ref, o_ref, *, tq, tk):
    i = pl.program_id(2)
    q = q_ref[...]

    def update(kstart, carry, diag_offset):
        m, l, acc = carry
        s = _dot_nt(q, k_ref[pl.ds(kstart, tk), :])
        if diag_offset is not None:
            row = lax.broadcasted_iota(jnp.int32, s.shape, 0)
            col = lax.broadcasted_iota(jnp.int32, s.shape, 1) + diag_offset
            s = jnp.where(col <= row, s, NEG)
        m_new = jnp.maximum(m, jnp.max(s, axis=-1, keepdims=True))
        alpha = jnp.exp2(m - m_new)
        p = jnp.exp2(s - m_new)
        l = alpha * l + jnp.sum(p, axis=-1, keepdims=True)
        acc = alpha * acc + _dot(p.astype(BF16), v_ref[pl.ds(kstart, tk), :])
        return m_new, l, acc

    init = (jnp.full((tq, 1), NEG, F32), jnp.zeros((tq, 1), F32), jnp.zeros((tq, HEAD), F32))
    carry = lax.fori_loop(0, i * (tq // tk), lambda kc, c: update(pl.multiple_of(kc * tk, tk), c, None), init)
    for c in range(tq // tk):
        carry = update(pl.multiple_of(i * tq + c * tk, tk), carry, c * tk)
    _, l, acc = carry
    o_ref[...] = (acc / l).astype(BF16)


def _mla_attn(q, k, v, *, B, S, tq, tk):
    T = B * S
    nq = S // tq
    return pl.pallas_call(
        functools.partial(_mla_attn_body, tq=tq, tk=tk),
        grid=(B, N_HEADS, nq),
        in_specs=[
            pl.BlockSpec((tq, MLA_PAD), lambda b, h, i: (b * nq + i, h)),
            pl.BlockSpec((S, MLA_PAD), lambda b, h, i: (b, h)),
            pl.BlockSpec((S, HEAD), lambda b, h, i: (b, h)),
        ],
        out_specs=pl.BlockSpec((tq, HEAD), lambda b, h, i: (b * nq + i, h)),
        out_shape=jax.ShapeDtypeStruct((T, HW), BF16),
        compiler_params=_cparams("parallel", "parallel", "arbitrary"),
        name="mla_attn",
    )(q, k, v)


def _band_body(q_ref, k_ref, kh_ref, v_ref, vh_ref, o_ref, st_ref, o32_ref, *, dil, nsub):
    rb = pl.program_id(1)
    h = pl.program_id(2)
    row = lax.broadcasted_iota(jnp.int32, (BAND, BAND), 0)
    col = lax.broadcasted_iota(jnp.int32, (BAND, BAND), 1)
    bias_prev = jnp.where(col >= row, 0.0, NEG).astype(F32)
    bias_cur = jnp.where(col <= row, 0.0, NEG).astype(F32)
    lane = lax.broadcasted_iota(jnp.int32, (BAND, LANES), 1)
    is_m = lane == h
    is_l = lane == h + N_HEADS

    @pl.when(h == 0)
    def _():
        st_ref[...] = jnp.ones_like(st_ref)

    nb = dil * nsub

    def blocks(ref):
        return ref[...].reshape(nb, BAND, HEAD)

    def prev_blocks(ref, halo_ref):
        if nsub == 1:
            return halo_ref[...]
        shifted = jnp.concatenate([halo_ref[...], ref[:, :(nsub - 1) * BAND, :]], axis=1)
        return shifted.reshape(nb, BAND, HEAD)

    bdot_nt = lambda a, b: lax.dot_general(a, b, (((2,), (2,)), ((0,), (0,))), preferred_element_type=F32)
    bdot = lambda a, b: lax.dot_general(a, b, (((2,), (1,)), ((0,), (0,))), preferred_element_type=F32)
    q = blocks(q_ref)
    blk = lax.broadcasted_iota(jnp.int32, (nb, BAND, BAND), 0)
    first = jnp.logical_and(blk % nsub == 0, rb == 0)
    s_prev = bdot_nt(q, prev_blocks(k_ref, kh_ref)) + jnp.where(first, NEG, bias_prev[None])
    s_cur = bdot_nt(q, blocks(k_ref)) + bias_cur[None]
    m = jnp.max(jnp.maximum(s_prev, s_cur), axis=-1, keepdims=True)
    e_prev = jnp.exp2(s_prev - m)
    e_cur = jnp.exp2(s_cur - m)
    l = jnp.sum(e_prev + e_cur, axis=-1, keepdims=True)
    o = bdot(e_prev.astype(BF16), prev_blocks(v_ref, vh_ref)) + bdot(e_cur.astype(BF16), blocks(v_ref))
    o = o / l
    if dil == 1:
        o_ref[...] = o.reshape(nb * BAND, HEAD).astype(BF16)
    for b in range(nb):
        r, c = divmod(b, nsub)
        tok = pl.ds(c * BAND * dil + r, BAND, stride=dil) if dil > 1 else slice(c * BAND, (c + 1) * BAND)
        if dil > 1:
            o32_ref[tok, :] = o[b]
        st_ref[tok, :] = jnp.where(is_m, m[b], jnp.where(is_l, l[b], st_ref[tok, :]))
    if dil > 1:
        o_ref[...] = o32_ref[...].astype(BF16)


def _band_attn(slabs, g, *, B, S):
    dil = DIL_GROUPS[g][1]
    L = S // dil
    chunk = min(S, 2048)
    rb_rows = chunk // dil
    nsub = rb_rows // BAND
    nrb = L // rb_rows
    blk = lambda slab: pl.BlockSpec((None, None, dil, rb_rows, HEAD), lambda b, rb, h: (slab, b, 0, rb, h))
    halo = lambda slab: pl.BlockSpec(
        (None, None, dil, BAND, HEAD), lambda b, rb, h: (slab, b, 0, jnp.maximum(rb * nsub - 1, 0), h))
    T = B * S
    return pl.pallas_call(
        functools.partial(_band_body, dil=dil, nsub=nsub),
        grid=(B, nrb, N_HEADS),
        in_specs=[blk(0), blk(1), halo(1), blk(2), halo(2)],
        out_specs=[
            pl.BlockSpec((chunk, HEAD), lambda b, rb, h: (b * nrb + rb, h)),
            pl.BlockSpec((chunk, LANES), lambda b, rb, h: (b * nrb + rb, 0)),
        ],
        out_shape=[
            jax.ShapeDtypeStruct((T, HW), BF16),
            jax.ShapeDtypeStruct((T, LANES), F32),
        ],
        scratch_shapes=[pltpu.VMEM((chunk, HEAD), F32)],
        compiler_params=_cparams("parallel", "parallel", "arbitrary"),
        name=f"band_attn_{g}",
    )(slabs, slabs, slabs, slabs, slabs)


def _merge_body(x_ref, om_ref, o0_ref, o1_ref, o2_ref, st0_ref, st1_ref, st2_ref,
                ga_ref, gb_ref, wa_ref, wb_ref, wo_ref, out_ref, od_ref):
    f = pl.program_id(1)

    @pl.when(f == 0)
    def _():
        m0, m1, m2 = st0_ref[...], st1_ref[...], st2_ref[...]
        l0, l1, l2 = (pltpu.roll(st, LANES - N_HEADS, 1) for st in (m0, m1, m2))
        mx = jnp.maximum(jnp.maximum(m0, m1), m2)
        w0 = l0 * jnp.exp2(m0 - mx)
        w1 = l1 * jnp.exp2(m1 - mx)
        w2 = l2 * jnp.exp2(m2 - mx)
        tot = w0 + w1 + w2
        w0, w1, w2 = w0 / tot, w1 / tot, w2 / tot
        tm = w0.shape[0]
        for h in range(N_HEADS):
            hs = slice(h * HEAD, (h + 1) * HEAD)
            bc = lambda w: jnp.broadcast_to(w[:, h:h + 1], (tm, HEAD))
            od = (bc(w0) * o0_ref[:, hs].astype(F32) + bc(w1) * o1_ref[:, hs].astype(F32)
                  + bc(w2) * o2_ref[:, hs].astype(F32))
            od_ref[:, hs] = od.astype(BF16)
        out_ref[...] = x_ref[...]

    a = _dot(om_ref[...], wa_ref[...])
    b = _dot(od_ref[...], wb_ref[...])
    merged = jax.nn.sigmoid(ga_ref[...].astype(F32)) * a + jax.nn.sigmoid(gb_ref[...].astype(F32)) * b
    out_ref[...] += _dot(merged.astype(BF16), wo_ref[...])


def _merge(x, o_mla, o_dil, st_dil, gates, wa, wb, wo, *, tm, tf):
    T, D = x.shape
    nf = D // tf
    row = lambda w: pl.BlockSpec((tm, w), lambda i, f: (i, 0))
    return pl.pallas_call(
        _merge_body,
        grid=(T // tm, nf),
        in_specs=[row(D), row(HW), row(HW), row(HW), row(HW)] + [row(LANES)] * 3 + [
            pl.BlockSpec((tm, tf), lambda i, f: (i, f)),
            pl.BlockSpec((tm, tf), lambda i, f: (i, nf + f)),
            pl.BlockSpec((HW, tf), lambda i, f: (0, f)),
            pl.BlockSpec((HW, tf), lambda i, f: (0, f)),
            pl.BlockSpec((tf, D), lambda i, f: (f, 0)),
        ],
        out_specs=row(D),
        out_shape=jax.ShapeDtypeStruct((T, D), F32),
        scratch_shapes=[pltpu.VMEM((tm, HW), BF16)],
        compiler_params=_cparams("parallel", "arbitrary"),
        name="merge",
    )(x, o_mla, *o_dil, *st_dil, gates, gates, wa, wb, wo)


def _ple_body(x_ref, g_ref, p_ref, wg_ref, wp_ref, o_ref):
    x = x_ref[...]
    h = _rmsnorm(x, g_ref[...]).astype(BF16)
    gate = jax.nn.sigmoid(_dot(h, wg_ref[...]))
    o_ref[...] = x + gate * _dot(p_ref[...].astype(BF16), wp_ref[...])


def _ple(x, g, p, wg, wp, *, tm):
    T, D = x.shape
    P = p.shape[1]
    return pl.pallas_call(
        _ple_body,
        grid=(T // tm,),
        in_specs=[
            pl.BlockSpec((tm, D), lambda i: (i, 0)),
            pl.BlockSpec((1, D), lambda i: (0, 0)),
            pl.BlockSpec((tm, P), lambda i: (i, 0)),
            pl.BlockSpec((D, D), lambda i: (0, 0)),
            pl.BlockSpec((P, D), lambda i: (0, 0)),
        ],
        out_specs=pl.BlockSpec((tm, D), lambda i: (i, 0)),
        out_shape=jax.ShapeDtypeStruct((T, D), F32),
        compiler_params=_cparams("parallel"),
        name="ple",
    )(x, g, p, wg, wp)


def _spread_rope(a):
    half = MLA_ROPE // 2
    z = jnp.zeros(a.shape[:-1] + (LANES // 2 - half,), a.dtype)
    return jnp.concatenate([a[..., :half], z, a[..., half:], z], axis=-1)


def _layer(x, p, i, g_ffn1, w1_gate, w1_up, w1_down, g_mix, w_in, g_cq, w_uq, g_ckv, w_ukv, g_q_mla, g_k_mla,
           g_q_dil, g_k_dil, w_br_mla, w_br_dil, w_o, g_ffn2, w2_gate, w2_up, w2_down, g_ple, w_ple_gate,
           w_ple_proj, tables, B, S):
    T, D = x.shape
    tm = min(T, 512)
    tf = min(w1_gate.shape[-1], 512)
    bf = lambda a: a.astype(BF16)
    row = lambda a: a.reshape(1, -1).astype(F32)
    (c_mla, s_mla), (c_dil, s_dil) = tables
    log2e = 1.4426950408889634

    x = _ffn(x, row(g_ffn1[i]), bf(w1_gate[i]), bf(w1_up[i]), bf(w1_down[i]), tm=tm, tf=tf)

    off_kr = Q_LORA + KV_LORA
    off_dil = off_kr + MLA_ROPE
    off_gate = off_dil + 3 * N_GROUPS * HW
    w = bf(w_in[i])
    wd = w[:, off_dil:off_gate].reshape(D, N_GROUPS, 3, N_HEADS, HEAD)
    wd_qk = _split_pair(wd[:, :, :2], DIL_ROT // 2)
    wd = jnp.concatenate([wd_qk, wd[:, :, 2:]], axis=2).reshape(D, 3 * N_GROUPS * HW)
    w_r = jnp.concatenate([w[:, :off_kr], _spread_rope(w[:, off_kr:off_dil]),
                           jnp.zeros((D, LAT_W - off_kr - LANES), BF16), wd, w[:, off_gate:]], axis=1)
    scale_dil = HEAD ** -0.5 * log2e
    pair = lambda v: _split_pair(v, DIL_ROT // 2)
    tile_heads = lambda v: jnp.tile(v, N_HEADS).reshape(1, HW)
    gd = jnp.stack([tile_heads(t) for g in range(N_GROUPS)
                    for t in (pair(g_q_dil[i, g]) * scale_dil, pair(g_k_dil[i, g]), jnp.ones((HEAD,), F32))],
                   axis=0)
    lat, d0, d1, d2, gates = _in_proj(x, row(g_mix[i]), w_r, gd, c_dil, s_dil, tm=tm, B=B, S=S)

    scale_mla = MLA_QK ** -0.5 * log2e
    wuq = w_uq[i].reshape(Q_LORA, N_HEADS, MLA_QK)
    wuq = jnp.concatenate([wuq[..., :HEAD], _spread_rope(wuq[..., HEAD:])], axis=-1)
    wuq = bf(wuq.reshape(Q_LORA, N_HEADS * MLA_PAD))
    wukv = bf(w_ukv[i].reshape(KV_LORA, N_HEADS, 2, HEAD).transpose(0, 2, 1, 3).reshape(KV_LORA, 2 * HW))
    gq, gk = g_q_mla[i] * scale_mla, g_k_mla[i]
    q, k, v = _mla_prep(lat, row(g_cq[i]), row(g_ckv[i]), wuq, wukv, row(gq[:HEAD]), row(_spread_rope(gq[HEAD:])),
                        row(gk[:HEAD]), row(_spread_rope(gk[HEAD:])), c_mla, s_mla, tm=tm)
    o_mla = _mla_attn(q, k, v, B=B, S=S, tq=min(S, 1024), tk=min(S, 512))

    outs = [_band_attn(d, g, B=B, S=S) for g, d in enumerate((d0, d1, d2))]
    o_dil, st_dil = zip(*outs)

    x = _merge(x, o_mla, o_dil, st_dil, gates, bf(w_br_mla[i]), bf(w_br_dil[i]), bf(w_o[i]),
               tm=tm, tf=min(D, 512))
    x = _ffn(x, row(g_ffn2[i]), bf(w2_gate[i]), bf(w2_up[i]), bf(w2_down[i]), tm=tm, tf=tf)
    x = _ple(x, row(g_ple[i]), p[i].reshape(T, -1), bf(w_ple_gate[i]), bf(w_ple_proj[i]), tm=tm)
    return x


def kernel(x, p, positions, g_ffn1, w1_gate, w1_up, w1_down, g_mix, w_in, g_cq, w_uq, g_ckv, w_ukv, g_q_mla,
           g_k_mla, g_q_dil, g_k_dil, w_br_mla, w_br_dil, w_o, g_ffn2, w2_gate, w2_up, w2_down, g_ple,
           w_ple_gate, w_ple_proj):
    B, S, D = x.shape
    T = B * S
    pos = positions.reshape(T)
    tables = (_rope_tables(pos, MLA_ROPE // 2), _rope_tables(pos, DIL_ROT // 2))
    xt = x.reshape(T, D)
    for i in range(g_ffn1.shape[0]):
        xt = _layer(xt, p, i, g_ffn1, w1_gate, w1_up, w1_down, g_mix, w_in, g_cq, w_uq, g_ckv, w_ukv,
                    g_q_mla, g_k_mla, g_q_dil, g_k_dil, w_br_mla, w_br_dil, w_o, g_ffn2, w2_gate, w2_up,
                    w2_down, g_ple, w_ple_gate, w_ple_proj, tables, B, S)
    return xt.reshape(B, S, D)
```

```python
import functools

import jax
import jax.numpy as jnp
from jax import lax
from jax.experimental import pallas as pl
from jax.experimental.pallas import tpu as pltpu

F32 = jnp.float32
BF16 = jnp.bfloat16

EPS = 1e-6
NEG = -1e30
ROPE_THETA = 500000.0

LANES = 128
N_HEADS = 8
HEAD = 128
MLA_ROPE = 64
MLA_QK = HEAD + MLA_ROPE
MLA_PAD = 256
Q_LORA = 512
KV_LORA = 256
LAT_W = 1024
DIL_GROUPS = ((128, 1), (512, 4), (2048, 16))
N_GROUPS = len(DIL_GROUPS)
DIL_ROT = 32
BAND = 128
HW = N_HEADS * HEAD

VMEM_LIMIT = 56 * 1024 * 1024


def _cparams(*sem):
    return pltpu.CompilerParams(dimension_semantics=sem, vmem_limit_bytes=VMEM_LIMIT)


def _rmsnorm(x, g):
    ms = jnp.mean(x * x, axis=-1, keepdims=True)
    return x * lax.rsqrt(ms + EPS) * g


def _dot(a, b):
    return jnp.dot(a, b, preferred_element_type=F32)


def _dot_nt(a, b):
    return lax.dot_general(a, b, (((1,), (1,)), ((), ())), preferred_element_type=F32)


def _rope_tile(y, c, s):
    return y * c + pltpu.roll(y, LANES // 2, 1) * s


def _trig_body(pos_ref, inv_ref, cos_ref, sin_ref):
    ang = pos_ref[...].astype(F32) * inv_ref[...]
    cos_ref[...] = jnp.cos(ang)
    sin_ref[...] = jnp.sin(ang)


def _rope_tables(pos, half):
    T = pos.shape[0]
    per_row = LANES // half
    rows = T // per_row
    inv = ROPE_THETA ** (-jnp.arange(half, dtype=F32) * 2.0 / (2 * half))
    pos_rep = jnp.broadcast_to(pos[:, None], (T, half)).reshape(rows, LANES)
    inv_rep = jnp.tile(inv, per_row).reshape(1, LANES)
    tr = min(rows, 512)
    cos, sin = pl.pallas_call(
        _trig_body,
        grid=(rows // tr,),
        in_specs=[pl.BlockSpec((tr, LANES), lambda i: (i, 0)), pl.BlockSpec((1, LANES), lambda i: (0, 0))],
        out_specs=[pl.BlockSpec((tr, LANES), lambda i: (i, 0))] * 2,
        out_shape=[jax.ShapeDtypeStruct((rows, LANES), F32)] * 2,
        compiler_params=_cparams("arbitrary"),
        name="rope_trig",
    )(pos_rep, inv_rep)
    cos = cos.reshape(T, half)
    sin = sin.reshape(T, half)
    rest = LANES // 2 - half
    one = jnp.ones((T, rest), F32)
    zero = jnp.zeros((T, rest), F32)
    c = jnp.concatenate([cos, one, cos, one], axis=1)
    s = jnp.concatenate([-sin, zero, sin, zero], axis=1)
    return c, s


def _split_pair(a, half):
    n = LANES // 2 - half
    return jnp.concatenate([a[..., :half], a[..., 2 * half:2 * half + n], a[..., half:2 * half],
                            a[..., 2 * half + n:]], axis=-1)


def _ffn_body(x_ref, g_ref, wg_ref, wu_ref, wd_ref, o_ref, h_ref):
    f = pl.program_id(1)

    @pl.when(f == 0)
    def _():
        x = x_ref[...]
        h_ref[...] = _rmsnorm(x, g_ref[...]).astype(BF16)
        o_ref[...] = x

    h = h_ref[...]
    a = _dot(h, wg_ref[...])
    u = _dot(h, wu_ref[...])
    act = (0.5 * a) * jax.nn.sigmoid(a) * u
    o_ref[...] += _dot(act.astype(BF16), wd_ref[...])


def _ffn(x, g, wg, wu, wd, *, tm, tf):
    T, D = x.shape
    F = wg.shape[1]
    return pl.pallas_call(
        _ffn_body,
        grid=(T // tm, F // tf),
        in_specs=[
            pl.BlockSpec((tm, D), lambda i, f: (i, 0)),
            pl.BlockSpec((1, D), lambda i, f: (0, 0)),
            pl.BlockSpec((D, tf), lambda i, f: (0, f)),
            pl.BlockSpec((D, tf), lambda i, f: (0, f)),
            pl.BlockSpec((tf, D), lambda i, f: (f, 0)),
        ],
        out_specs=pl.BlockSpec((tm, D), lambda i, f: (i, 0)),
        out_shape=jax.ShapeDtypeStruct((T, D), F32),
        scratch_shapes=[pltpu.VMEM((tm, D), BF16)],
        compiler_params=_cparams("parallel", "arbitrary"),
        name="ffn",
    )(x, g, wg, wu, wd)


def _proj_body(x_ref, g_ref, w_ref, gd_ref, c_ref, s_ref, lat_ref, d0_ref, d1_ref, d2_ref, gate_ref,
               h_ref, y_ref, z_ref, *, n_dil):
    j = pl.program_id(1)
    tm = y_ref.shape[0]

    @pl.when(j == 0)
    def _():
        h_ref[...] = _rmsnorm(x_ref[...], g_ref[...]).astype(BF16)

    y_ref[...] = _dot(h_ref[...], w_ref[...])

    @pl.when(j == 0)
    def _():
        lat_ref[...] = y_ref[...]

    is_dil = jnp.logical_and(j >= 1, j <= n_dil)
    is_v = (j - 1) % 3 == 2

    @pl.when(jnp.logical_and(is_dil, is_v))
    def _():
        for h in range(N_HEADS):
            z_ref[h] = y_ref[:, h * HEAD:(h + 1) * HEAD]

    @pl.when(jnp.logical_and(is_dil, jnp.logical_not(is_v)))
    def _():
        c = c_ref[...]
        s = s_ref[...]
        pair = 2 * HEAD
        same_head = (lax.broadcasted_iota(jnp.int32, (pair, pair), 0) // HEAD
                     == lax.broadcasted_iota(jnp.int32, (pair, pair), 1) // HEAD)
        ones_bd = jnp.where(same_head, 1.0, 0.0).astype(BF16)
        for hp in range(N_HEADS // 2):
            sl = slice(hp * pair, (hp + 1) * pair)
            y = y_ref[:, sl]
            ss = _dot((y * y).astype(BF16), ones_bd)
            yn = y * lax.rsqrt(ss * (1.0 / HEAD) + EPS) * gd_ref[:, sl]
            for t in range(2):
                z_ref[2 * hp + t] = _rope_tile(yn[:, t * HEAD:(t + 1) * HEAD], c, s)

    for g, d_ref in enumerate((d0_ref, d1_ref, d2_ref)):
        dil = DIL_GROUPS[g][1]

        @pl.when(jnp.logical_and(j >= 1 + 3 * g, j <= 3 + 3 * g))
        def _(d_ref=d_ref, dil=dil):
            for h in range(N_HEADS):
                for r in range(dil):
                    rows = pl.ds(r, tm // dil, stride=dil) if dil > 1 else slice(None)
                    d_ref[r, :, h * HEAD:(h + 1) * HEAD] = z_ref[h, rows, :].astype(BF16)

    @pl.when(j > n_dil)
    def _():
        gate_ref[...] = y_ref[...].astype(BF16)


def _in_proj(x, g, w, gd, c, s, *, tm, B, S):
    T, D = x.shape
    n_dil = 3 * N_GROUPS
    n_gate = (2 * D) // LAT_W
    nj = 1 + n_dil + n_gate
    assert w.shape[1] == nj * LAT_W
    per_b = S // tm

    def dil_spec(g):
        dil = DIL_GROUPS[g][1]
        return pl.BlockSpec((None, None, dil, tm // dil, HW),
                            lambda i, j: (jnp.clip(j - 1 - 3 * g, 0, 2), i // per_b, 0, i % per_b, 0))

    def dil_shape(g):
        dil = DIL_GROUPS[g][1]
        return jax.ShapeDtypeStruct((3, B, dil, S // dil, HW), BF16)

    tab = pl.BlockSpec((tm, LANES), lambda i, j: (i, 0))
    return pl.pallas_call(
        functools.partial(_proj_body, n_dil=n_dil),
        grid=(T // tm, nj),
        in_specs=[
            pl.BlockSpec((tm, D), lambda i, j: (i, 0)),
            pl.BlockSpec((1, D), lambda i, j: (0, 0)),
            pl.BlockSpec((D, LAT_W), lambda i, j: (0, j)),
            pl.BlockSpec((None, 1, HW), lambda i, j: (jnp.clip(j - 1, 0, n_dil - 1), 0, 0)),
            tab, tab,
        ],
        out_specs=[
            pl.BlockSpec((tm, LAT_W), lambda i, j: (i, 0)),
            dil_spec(0), dil_spec(1), dil_spec(2),
            pl.BlockSpec((tm, LAT_W), lambda i, j: (i, jnp.clip(j - 1 - n_dil, 0, n_gate - 1))),
        ],
        out_shape=[
            jax.ShapeDtypeStruct((T, LAT_W), F32),
            dil_shape(0), dil_shape(1), dil_shape(2),
            jax.ShapeDtypeStruct((T, 2 * D), BF16),
        ],
        scratch_shapes=[pltpu.VMEM((tm, D), BF16), pltpu.VMEM((tm, LAT_W), F32),
                        pltpu.VMEM((N_HEADS, tm, HEAD), F32)],
        compiler_params=_cparams("parallel", "arbitrary"),
        name="in_proj",
    )(x, g, w, gd, c, s)


def _mla_prep_body(lat_ref, gcq_ref, gckv_ref, wuq_ref, wukv_ref, gqn_ref, gqr_ref, gkn_ref, gkr_ref,
                   c_ref, s_ref, q_ref, k_ref, v_ref):
    c = c_ref[...]
    s = s_ref[...]
    cq = _rmsnorm(lat_ref[:, :Q_LORA], gcq_ref[...]).astype(BF16)
    ckv = _rmsnorm(lat_ref[:, Q_LORA:Q_LORA + KV_LORA], gckv_ref[...]).astype(BF16)
    qa = _dot(cq, wuq_ref[...])
    kva = _dot(ckv, wukv_ref[...])
    kr = lat_ref[:, Q_LORA + KV_LORA:Q_LORA + KV_LORA + LANES]
    kr_ss = jnp.sum(kr * kr, axis=-1, keepdims=True)
    kr_rot = _rope_tile(kr * gkr_ref[...], c, s)
    for h in range(N_HEADS):
        lo = h * MLA_PAD
        qn = qa[:, lo:lo + HEAD]
        qr = qa[:, lo + HEAD:lo + MLA_PAD]
        ss = jnp.sum(qn * qn, axis=-1, keepdims=True) + jnp.sum(qr * qr, axis=-1, keepdims=True)
        r = lax.rsqrt(ss * (1.0 / MLA_QK) + EPS)
        q_ref[:, lo:lo + HEAD] = (qn * r * gqn_ref[...]).astype(BF16)
        q_ref[:, lo + HEAD:lo + MLA_PAD] = (_rope_tile(qr * gqr_ref[...], c, s) * r).astype(BF16)
        kn = kva[:, h * HEAD:(h + 1) * HEAD]
        ssk = jnp.sum(kn * kn, axis=-1, keepdims=True) + kr_ss
        rk = lax.rsqrt(ssk * (1.0 / MLA_QK) + EPS)
        k_ref[:, lo:lo + HEAD] = (kn * rk * gkn_ref[...]).astype(BF16)
        k_ref[:, lo + HEAD:lo + MLA_PAD] = (kr_rot * rk).astype(BF16)
    v_ref[...] = kva[:, HW:].astype(BF16)


def _mla_prep(lat, gcq, gckv, wuq, wukv, gqn, gqr, gkn, gkr, c, s, *, tm):
    T = lat.shape[0]
    row = lambda w: pl.BlockSpec((tm, w), lambda i: (i, 0))
    full = lambda a: pl.BlockSpec(a.shape, lambda i: (0, 0))
    return pl.pallas_call(
        _mla_prep_body,
        grid=(T // tm,),
        in_specs=[row(LAT_W), full(gcq), full(gckv), full(wuq), full(wukv), full(gqn), full(gqr), full(gkn),
                  full(gkr), row(LANES), row(LANES)],
        out_specs=[row(N_HEADS * MLA_PAD), row(N_HEADS * MLA_PAD), row(HW)],
        out_shape=[
            jax.ShapeDtypeStruct((T, N_HEADS * MLA_PAD), BF16),
            jax.ShapeDtypeStruct((T, N_HEADS * MLA_PAD), BF16),
            jax.ShapeDtypeStruct((T, HW), BF16),
        ],
        compiler_params=_cparams("parallel"),
        name="mla_prep",
    )(lat, gcq, gckv, wuq, wukv, gqn, gqr, gkn, gkr, c, s)


def _mla_attn_body(q_ref, k_ref, v_ref, o_ref, *, tq, tk):
    i = pl.program_id(2)
    q = q_ref[...]

    def update(kstart, carry, diag_offset):
        m, l, acc = carry
        s = _dot_nt(q, k_ref[pl.ds(kstart, tk), :])
        if diag_offset is not None:
            row = lax.broadcasted_iota(jnp.int32, s.shape, 0)
            col = lax.broadcasted_iota(jnp.int32, s.shape, 1) + diag_offset
            s = jnp.where(col <= row, s, NEG)
        m_new = jnp.maximum(m, jnp.max(s, axis=-1, keepdims=True))
        alpha = jnp.exp2(m - m_new)
        p = jnp.exp2(s - m_new)
        l = alpha * l + jnp.sum(p, axis=-1, keepdims=True)
        acc = alpha * acc + _dot(p.astype(BF16), v_ref[pl.ds(kstart, tk), :])
        return m_new, l, acc

    init = (jnp.full((tq, 1), NEG, F32), jnp.zeros((tq, 1), F32), jnp.zeros((tq, HEAD), F32))
    carry = lax.fori_loop(0, i * (tq // tk), lambda kc, c: update(pl.multiple_of(kc * tk, tk), c, None), init)
    for c in range(tq // tk):
        carry = update(pl.multiple_of(i * tq + c * tk, tk), carry, c * tk)
    _, l, acc = carry
    o_ref[...] = (acc / l).astype(BF16)


def _mla_attn(q, k, v, *, B, S, tq, tk):
    T = B * S
    nq = S // tq
    return pl.pallas_call(
        functools.partial(_mla_attn_body, tq=tq, tk=tk),
        grid=(B, N_HEADS, nq),
        in_specs=[
            pl.BlockSpec((tq, MLA_PAD), lambda b, h, i: (b * nq + i, h)),
            pl.BlockSpec((S, MLA_PAD), lambda b, h, i: (b, h)),
            pl.BlockSpec((S, HEAD), lambda b, h, i: (b, h)),
        ],
        out_specs=pl.BlockSpec((tq, HEAD), lambda b, h, i: (b * nq + i, h)),
        out_shape=jax.ShapeDtypeStruct((T, HW), BF16),
        compiler_params=_cparams("parallel", "parallel", "arbitrary"),
        name="mla_attn",
    )(q, k, v)


def _band_body(q_ref, k_ref, kh_ref, v_ref, vh_ref, o_ref, st_ref, o32_ref, *, dil, nsub):
    rb = pl.program_id(1)
    h = pl.program_id(2)
    row = lax.broadcasted_iota(jnp.int32, (BAND, BAND), 0)
    col = lax.broadcasted_iota(jnp.int32, (BAND, BAND), 1)
    bias_prev = jnp.where(col >= row, 0.0, NEG).astype(F32)
    bias_cur = jnp.where(col <= row, 0.0, NEG).astype(F32)
    lane = lax.broadcasted_iota(jnp.int32, (BAND, LANES), 1)
    is_m = lane == h
    is_l = lane == h + N_HEADS

    @pl.when(h == 0)
    def _():
        st_ref[...] = jnp.ones_like(st_ref)

    nb = dil * nsub

    def blocks(ref):
        return ref[...].reshape(nb, BAND, HEAD)

    def prev_blocks(ref, halo_ref):
        if nsub == 1:
            return halo_ref[...]
        shifted = jnp.concatenate([halo_ref[...], ref[:, :(nsub - 1) * BAND, :]], axis=1)
        return shifted.reshape(nb, BAND, HEAD)

    bdot_nt = lambda a, b: lax.dot_general(a, b, (((2,), (2,)), ((0,), (0,))), preferred_element_type=F32)
    bdot = lambda a, b: lax.dot_general(a, b, (((2,), (1,)), ((0,), (0,))), preferred_element_type=F32)
    q = blocks(q_ref)
    blk = lax.broadcasted_iota(jnp.int32, (nb, BAND, BAND), 0)
    first = jnp.logical_and(blk % nsub == 0, rb == 0)
    s_prev = bdot_nt(q, prev_blocks(k_ref, kh_ref)) + jnp.where(first, NEG, bias_prev[None])
    s_cur = bdot_nt(q, blocks(k_ref)) + bias_cur[None]
    m = jnp.max(jnp.maximum(s_prev, s_cur), axis=-1, keepdims=True)
    e_prev = jnp.exp2(s_prev - m)
    e_cur = jnp.exp2(s_cur - m)
    l = jnp.sum(e_prev + e_cur, axis=-1, keepdims=True)
    o = bdot(e_prev.astype(BF16), prev_blocks(v_ref, vh_ref)) + bdot(e_cur.astype(BF16), blocks(v_ref))
    o = o / l
    if dil == 1:
        o_ref[...] = o.reshape(nb * BAND, HEAD).astype(BF16)
    for b in range(nb):
        r, c = divmod(b, nsub)
        tok = pl.ds(c * BAND * dil + r, BAND, stride=dil) if dil > 1 else slice(c * BAND, (c + 1) * BAND)
        if dil > 1:
            o32_ref[tok, :] = o[b]
        st_ref[tok, :] = jnp.where(is_m, m[b], jnp.where(is_l, l[b], st_ref[tok, :]))
    if dil > 1:
        o_ref[...] = o32_ref[...].astype(BF16)


def _band_attn(slabs, g, *, B, S):
    dil = DIL_GROUPS[g][1]
    L = S // dil
    chunk = min(S, 2048)
    rb_rows = chunk // dil
    nsub = rb_rows // BAND
    nrb = L // rb_rows
    blk = lambda slab: pl.BlockSpec((None, None, dil, rb_rows, HEAD), lambda b, rb, h: (slab, b, 0, rb, h))
    halo = lambda slab: pl.BlockSpec(
        (None, None, dil, BAND, HEAD), lambda b, rb, h: (slab, b, 0, jnp.maximum(rb * nsub - 1, 0), h))
    T = B * S
    return pl.pallas_call(
        functools.partial(_band_body, dil=dil, nsub=nsub),
        grid=(B, nrb, N_HEADS),
        in_specs=[blk(0), blk(1), halo(1), blk(2), halo(2)],
        out_specs=[
            pl.BlockSpec((chunk, HEAD), lambda b, rb, h: (b * nrb + rb, h)),
            pl.BlockSpec((chunk, LANES), lambda b, rb, h: (b * nrb + rb, 0)),
        ],
        out_shape=[
            jax.ShapeDtypeStruct((T, HW), BF16),
            jax.ShapeDtypeStruct((T, LANES), F32),
        ],
        scratch_shapes=[pltpu.VMEM((chunk, HEAD), F32)],
        compiler_params=_cparams("parallel", "parallel", "arbitrary"),
        name=f"band_attn_{g}",
    )(slabs, slabs, slabs, slabs, slabs)


def _merge_body(x_ref, om_ref, o0_ref, o1_ref, o2_ref, st0_ref, st1_ref, st2_ref,
                ga_ref, gb_ref, wa_ref, wb_ref, wo_ref, out_ref, od_ref):
    f = pl.program_id(1)

    @pl.when(f == 0)
    def _():
        m0, m1, m2 = st0_ref[...], st1_ref[...], st2_ref[...]
        l0, l1, l2 = (pltpu.roll(st, LANES - N_HEADS, 1) for st in (m0, m1, m2))
        mx = jnp.maximum(jnp.maximum(m0, m1), m2)
        w0 = l0 * jnp.exp2(m0 - mx)
        w1 = l1 * jnp.exp2(m1 - mx)
        w2 = l2 * jnp.exp2(m2 - mx)
        tot = w0 + w1 + w2
        w0, w1, w2 = w0 / tot, w1 / tot, w2 / tot
        tm = w0.shape[0]
        for h in range(N_HEADS):
            hs = slice(h * HEAD, (h + 1) * HEAD)
            bc = lambda w: jnp.broadcast_to(w[:, h:h + 1], (tm, HEAD))
            od = (bc(w0) * o0_ref[:, hs].astype(F32) + bc(w1) * o1_ref[:, hs].astype(F32)
                  + bc(w2) * o2_ref[:, hs].astype(F32))
            od_ref[:, hs] = od.astype(BF16)
        out_ref[...] = x_ref[...]

    a = _dot(om_ref[...], wa_ref[...])
    b = _dot(od_ref[...], wb_ref[...])
    merged = jax.nn.sigmoid(ga_ref[...].astype(F32)) * a + jax.nn.sigmoid(gb_ref[...].astype(F32)) * b
    out_ref[...] += _dot(merged.astype(BF16), wo_ref[...])


def _merge(x, o_mla, o_dil, st_dil, gates, wa, wb, wo, *, tm, tf):
    T, D = x.shape
    nf = D // tf
    row = lambda w: pl.BlockSpec((tm, w), lambda i, f: (i, 0))
    return pl.pallas_call(
        _merge_body,
        grid=(T // tm, nf),
        in_specs=[row(D), row(HW), row(HW), row(HW), row(HW)] + [row(LANES)] * 3 + [
            pl.BlockSpec((tm, tf), lambda i, f: (i, f)),
            pl.BlockSpec((tm, tf), lambda i, f: (i, nf + f)),
            pl.BlockSpec((HW, tf), lambda i, f: (0, f)),
            pl.BlockSpec((HW, tf), lambda i, f: (0, f)),
            pl.BlockSpec((tf, D), lambda i, f: (f, 0)),
        ],
        out_specs=row(D),
        out_shape=jax.ShapeDtypeStruct((T, D), F32),
        scratch_shapes=[pltpu.VMEM((tm, HW), BF16)],
        compiler_params=_cparams("parallel", "arbitrary"),
        name="merge",
    )(x, o_mla, *o_dil, *st_dil, gates, gates, wa, wb, wo)


def _ple_body(x_ref, g_ref, p_ref, wg_ref, wp_ref, o_ref):
    x = x_ref[...]
    h = _rmsnorm(x, g_ref[...]).astype(BF16)
    gate = jax.nn.sigmoid(_dot(h, wg_ref[...]))
    o_ref[...] = x + gate * _dot(p_ref[...].astype(BF16), wp_ref[...])


def _ple(x, g, p, wg, wp, *, tm):
    T, D = x.shape
    P = p.shape[1]
    return pl.pallas_call(
        _ple_body,
        grid=(T // tm,),
        in_specs=[
            pl.BlockSpec((tm, D), lambda i: (i, 0)),
            pl.BlockSpec((1, D), lambda i: (0, 0)),
            pl.BlockSpec((tm, P), lambda i: (i, 0)),
            pl.BlockSpec((D, D), lambda i: (0, 0)),
            pl.BlockSpec((P, D), lambda i: (0, 0)),
        ],
        out_specs=pl.BlockSpec((tm, D), lambda i: (i, 0)),
        out_shape=jax.ShapeDtypeStruct((T, D), F32),
        compiler_params=_cparams("parallel"),
        name="ple",
    )(x, g, p, wg, wp)


def _spread_rope(a):
    half = MLA_ROPE // 2
    z = jnp.zeros(a.shape[:-1] + (LANES // 2 - half,), a.dtype)
    return jnp.concatenate([a[..., :half], z, a[..., half:], z], axis=-1)


def _layer(x, p, i, g_ffn1, w1_gate, w1_up, w1_down, g_mix, w_in, g_cq, w_uq, g_ckv, w_ukv, g_q_mla, g_k_mla,
           g_q_dil, g_k_dil, w_br_mla, w_br_dil, w_o, g_ffn2, w2_gate, w2_up, w2_down, g_ple, w_ple_gate,
           w_ple_proj, tables, B, S):
    T, D = x.shape
    tm = min(T, 512)
    tf = min(w1_gate.shape[-1], 512)
    bf = lambda a: a.astype(BF16)
    row = lambda a: a.reshape(1, -1).astype(F32)
    (c_mla, s_mla), (c_dil, s_dil) = tables
    log2e = 1.4426950408889634

    x = _ffn(x, row(g_ffn1[i]), bf(w1_gate[i]), bf(w1_up[i]), bf(w1_down[i]), tm=tm, tf=tf)

    off_kr = Q_LORA + KV_LORA
    off_dil = off_kr + MLA_ROPE
    off_gate = off_dil + 3 * N_GROUPS * HW
    w = bf(w_in[i])
    wd = w[:, off_dil:off_gate].reshape(D, N_GROUPS, 3, N_HEADS, HEAD)
    wd_qk = _split_pair(wd[:, :, :2], DIL_ROT // 2)
    wd = jnp.concatenate([wd_qk, wd[:, :, 2:]], axis=2).reshape(D, 3 * N_GROUPS * HW)
    w_r = jnp.concatenate([w[:, :off_kr], _spread_rope(w[:, off_kr:off_dil]),
                           jnp.zeros((D, LAT_W - off_kr - LANES), BF16), wd, w[:, off_gate:]], axis=1)
    scale_dil = HEAD ** -0.5 * log2e
    pair = lambda v: _split_pair(v, DIL_ROT // 2)
    tile_heads = lambda v: jnp.tile(v, N_HEADS).reshape(1, HW)
    gd = jnp.stack([tile_heads(t) for g in range(N_GROUPS)
                    for t in (pair(g_q_dil[i, g]) * scale_dil, pair(g_k_dil[i, g]), jnp.ones((HEAD,), F32))],
                   axis=0)
    lat, d0, d1, d2, gates = _in_proj(x, row(g_mix[i]), w_r, gd, c_dil, s_dil, tm=tm, B=B, S=S)

    scale_mla = MLA_QK ** -0.5 * log2e
    wuq = w_uq[i].reshape(Q_LORA, N_HEADS, MLA_QK)
    wuq = jnp.concatenate([wuq[..., :HEAD], _spread_rope(wuq[..., HEAD:])], axis=-1)
    wuq = bf(wuq.reshape(Q_LORA, N_HEADS * MLA_PAD))
    wukv = bf(w_ukv[i].reshape(KV_LORA, N_HEADS, 2, HEAD).transpose(0, 2, 1, 3).reshape(KV_LORA, 2 * HW))
    gq, gk = g_q_mla[i] * scale_mla, g_k_mla[i]
    q, k, v = _mla_prep(lat, row(g_cq[i]), row(g_ckv[i]), wuq, wukv, row(gq[:HEAD]), row(_spread_rope(gq[HEAD:])),
                        row(gk[:HEAD]), row(_spread_rope(gk[HEAD:])), c_mla, s_mla, tm=tm)
    o_mla = _mla_attn(q, k, v, B=B, S=S, tq=min(S, 1024), tk=min(S, 1024))

    outs = [_band_attn(d, g, B=B, S=S) for g, d in enumerate((d0, d1, d2))]
    o_dil, st_dil = zip(*outs)

    x = _merge(x, o_mla, o_dil, st_dil, gates, bf(w_br_mla[i]), bf(w_br_dil[i]), bf(w_o[i]),
               tm=tm, tf=min(D, 1024))
    x = _ffn(x, row(g_ffn2[i]), bf(w2_gate[i]), bf(w2_up[i]), bf(w2_down[i]), tm=tm, tf=tf)
    x = _ple(x, row(g_ple[i]), p[i].reshape(T, -1), bf(w_ple_gate[i]), bf(w_ple_proj[i]), tm=tm)
    return x


def kernel(x, p, positions, g_ffn1, w1_gate, w1_up, w1_down, g_mix, w_in, g_cq, w_uq, g_ckv, w_ukv, g_q_mla,
           g_k_mla, g_q_dil, g_k_dil, w_br_mla, w_br_dil, w_o, g_ffn2, w2_gate, w2_up, w2_down, g_ple,
           w_ple_gate, w_ple_proj):
    B, S, D = x.shape
    T = B * S
    pos = positions.reshape(T)
    tables = (_rope_tables(pos, MLA_ROPE // 2), _rope_tables(pos, DIL_ROT // 2))
    xt = x.reshape(T, D)
    for i in range(g_ffn1.shape[0]):
        xt = _layer(xt, p, i, g_ffn1, w1_gate, w1_up, w1_down, g_mix, w_in, g_cq, w_uq, g_ckv, w_ukv,
                    g_q_mla, g_k_mla, g_q_dil, g_k_dil, w_br_mla, w_br_dil, w_o, g_ffn2, w2_gate, w2_up,
                    w2_down, g_ple, w_ple_gate, w_ple_proj, tables, B, S)
    return xt.reshape(B, S, D)
```

```python
import functools

import jax
import jax.numpy as jnp
from jax import lax
from jax.experimental import pallas as pl
from jax.experimental.pallas import tpu as pltpu

F32 = jnp.float32
BF16 = jnp.bfloat16

EPS = 1e-6
NEG = -1e30
ROPE_THETA = 500000.0

LANES = 128
N_HEADS = 8
HEAD = 128
MLA_ROPE = 64
MLA_QK = HEAD + MLA_ROPE
MLA_PAD = 256
Q_LORA = 512
KV_LORA = 256
LAT_W = 1024
DIL_GROUPS = ((128, 1), (512, 4), (2048, 16))
N_GROUPS = len(DIL_GROUPS)
DIL_ROT = 32
BAND = 128
HW = N_HEADS * HEAD

VMEM_LIMIT = 56 * 1024 * 1024


def _cparams(*sem):
    return pltpu.CompilerParams(dimension_semantics=sem, vmem_limit_bytes=VMEM_LIMIT)


def _rmsnorm(x, g):
    ms = jnp.mean(x * x, axis=-1, keepdims=True)
    return x * lax.rsqrt(ms + EPS) * g


def _dot(a, b):
    return jnp.dot(a, b, preferred_element_type=F32)


def _dot_nt(a, b):
    return lax.dot_general(a, b, (((1,), (1,)), ((), ())), preferred_element_type=F32)


def _rope_tile(y, c, s):
    return y * c + pltpu.roll(y, LANES // 2, 1) * s


def _trig_body(pos_ref, inv_ref, cm_ref, sm_ref, cd_ref, sd_ref):
    hd, hm, mid = DIL_ROT // 2, MLA_ROPE // 2, LANES // 2
    ang = pos_ref[...].astype(F32) * inv_ref[...]
    cosv = jnp.cos(ang)
    sinv = jnp.sin(ang)
    lane = lax.broadcasted_iota(jnp.int32, ang.shape, 1)

    def tables(cv, sv, half):
        x1 = lane < half
        x2 = jnp.logical_and(lane >= mid, lane < mid + half)
        c = jnp.where(x1, cv, jnp.where(x2, cv, 1.0))
        s = jnp.where(x1, -sv, jnp.where(x2, sv, 0.0))
        return c, s

    cd_ref[...], sd_ref[...] = tables(cosv, sinv, hd)
    cm_ref[...], sm_ref[...] = tables(pltpu.roll(cosv, LANES - hd, 1), pltpu.roll(sinv, LANES - hd, 1), hm)


def _rope_tables(pos):
    T = pos.shape[0]
    hd, hm, mid = DIL_ROT // 2, MLA_ROPE // 2, LANES // 2
    inv_d = ROPE_THETA ** (-jnp.arange(hd, dtype=F32) * 2.0 / (2 * hd))
    inv_m = ROPE_THETA ** (-jnp.arange(hm, dtype=F32) * 2.0 / (2 * hm))
    pad = jnp.zeros((mid - hd - hm,), F32)
    inv = jnp.concatenate([inv_d, inv_m, pad, inv_d, inv_m, pad]).reshape(1, LANES)
    tr = min(T, 512)
    tab = pl.BlockSpec((tr, LANES), lambda i: (i, 0))
    cm, sm, cd, sd = pl.pallas_call(
        _trig_body,
        grid=(T // tr,),
        in_specs=[pl.BlockSpec((tr, 1), lambda i: (i, 0)), pl.BlockSpec((1, LANES), lambda i: (0, 0))],
        out_specs=[tab] * 4,
        out_shape=[jax.ShapeDtypeStruct((T, LANES), F32)] * 4,
        compiler_params=_cparams("parallel"),
        name="rope_trig",
    )(pos.reshape(T, 1), inv)
    return (cm, sm), (cd, sd)


def _split_pair(a, half):
    n = LANES // 2 - half
    return jnp.concatenate([a[..., :half], a[..., 2 * half:2 * half + n], a[..., half:2 * half],
                            a[..., 2 * half + n:]], axis=-1)


def _ffn_body(x_ref, g_ref, wg_ref, wu_ref, wd_ref, o_ref, h_ref):
    f = pl.program_id(1)

    @pl.when(f == 0)
    def _():
        x = x_ref[...]
        h_ref[...] = _rmsnorm(x, g_ref[...]).astype(BF16)
        o_ref[...] = x

    h = h_ref[...]
    a = _dot(h, wg_ref[...])
    u = _dot(h, wu_ref[...])
    act = (0.5 * a) * jax.nn.sigmoid(a) * u
    o_ref[...] += _dot(act.astype(BF16), wd_ref[...])


def _ffn(x, g, wg, wu, wd, *, tm, tf):
    T, D = x.shape
    F = wg.shape[1]
    return pl.pallas_call(
        _ffn_body,
        grid=(T // tm, F // tf),
        in_specs=[
            pl.BlockSpec((tm, D), lambda i, f: (i, 0)),
            pl.BlockSpec((1, D), lambda i, f: (0, 0)),
            pl.BlockSpec((D, tf), lambda i, f: (0, f)),
            pl.BlockSpec((D, tf), lambda i, f: (0, f)),
            pl.BlockSpec((tf, D), lambda i, f: (f, 0)),
        ],
        out_specs=pl.BlockSpec((tm, D), lambda i, f: (i, 0)),
        out_shape=jax.ShapeDtypeStruct((T, D), F32),
        scratch_shapes=[pltpu.VMEM((tm, D), BF16)],
        compiler_params=_cparams("parallel", "arbitrary"),
        name="ffn",
    )(x, g, wg, wu, wd)


def _proj_body(x_ref, g_ref, w_ref, gd_ref, c_ref, s_ref, lat_ref, d0_ref, d1_ref, d2_ref, gate_ref,
               h_ref, y_ref, z_ref, *, n_dil):
    j = pl.program_id(1)
    tm = y_ref.shape[0]

    @pl.when(j == 0)
    def _():
        h_ref[...] = _rmsnorm(x_ref[...], g_ref[...]).astype(BF16)

    y_ref[...] = _dot(h_ref[...], w_ref[...])

    @pl.when(j == 0)
    def _():
        lat_ref[...] = y_ref[...]

    is_dil = jnp.logical_and(j >= 1, j <= n_dil)
    is_v = (j - 1) % 3 == 2

    @pl.when(jnp.logical_and(is_dil, is_v))
    def _():
        for h in range(N_HEADS):
            z_ref[h] = y_ref[:, h * HEAD:(h + 1) * HEAD]

    @pl.when(jnp.logical_and(is_dil, jnp.logical_not(is_v)))
    def _():
        c = c_ref[...]
        s = s_ref[...]
        pair = 2 * HEAD
        same_head = (lax.broadcasted_iota(jnp.int32, (pair, pair), 0) // HEAD
                     == lax.broadcasted_iota(jnp.int32, (pair, pair), 1) // HEAD)
        ones_bd = jnp.where(same_head, 1.0, 0.0).astype(BF16)
        for hp in range(N_HEADS // 2):
            sl = slice(hp * pair, (hp + 1) * pair)
            y = y_ref[:, sl]
            ss = _dot((y * y).astype(BF16), ones_bd)
            yn = y * lax.rsqrt(ss * (1.0 / HEAD) + EPS) * gd_ref[:, sl]
            for t in range(2):
                z_ref[2 * hp + t] = _rope_tile(yn[:, t * HEAD:(t + 1) * HEAD], c, s)

    for g, d_ref in enumerate((d0_ref, d1_ref, d2_ref)):
        dil = DIL_GROUPS[g][1]

        @pl.when(jnp.logical_and(j >= 1 + 3 * g, j <= 3 + 3 * g))
        def _(d_ref=d_ref, dil=dil):
            for h in range(N_HEADS):
                for r in range(dil):
                    rows = pl.ds(r, tm // dil, stride=dil) if dil > 1 else slice(None)
                    d_ref[r, :, h * HEAD:(h + 1) * HEAD] = z_ref[h, rows, :].astype(BF16)

    @pl.when(j > n_dil)
    def _():
        gate_ref[...] = y_ref[...].astype(BF16)


def _in_proj(x, g, w, gd, c, s, *, tm, B, S):
    T, D = x.shape
    n_dil = 3 * N_GROUPS
    n_gate = (2 * D) // LAT_W
    nj = 1 + n_dil + n_gate
    assert w.shape[1] == nj * LAT_W
    per_b = S // tm

    def dil_spec(g):
        dil = DIL_GROUPS[g][1]
        return pl.BlockSpec((None, None, dil, tm // dil, HW),
                            lambda i, j: (jnp.clip(j - 1 - 3 * g, 0, 2), i // per_b, 0, i % per_b, 0))

    def dil_shape(g):
        dil = DIL_GROUPS[g][1]
        return jax.ShapeDtypeStruct((3, B, dil, S // dil, HW), BF16)

    tab = pl.BlockSpec((tm, LANES), lambda i, j: (i, 0))
    return pl.pallas_call(
        functools.partial(_proj_body, n_dil=n_dil),
        grid=(T // tm, nj),
        in_specs=[
            pl.BlockSpec((tm, D), lambda i, j: (i, 0)),
            pl.BlockSpec((1, D), lambda i, j: (0, 0)),
            pl.BlockSpec((D, LAT_W), lambda i, j: (0, j)),
            pl.BlockSpec((None, 1, HW), lambda i, j: (jnp.clip(j - 1, 0, n_dil - 1), 0, 0)),
            tab, tab,
        ],
        out_specs=[
            pl.BlockSpec((tm, LAT_W), lambda i, j: (i, 0)),
            dil_spec(0), dil_spec(1), dil_spec(2),
            pl.BlockSpec((tm, LAT_W), lambda i, j: (i, jnp.clip(j - 1 - n_dil, 0, n_gate - 1))),
        ],
        out_shape=[
            jax.ShapeDtypeStruct((T, LAT_W), F32),
            dil_shape(0), dil_shape(1), dil_shape(2),
            jax.ShapeDtypeStruct((T, 2 * D), BF16),
        ],
        scratch_shapes=[pltpu.VMEM((tm, D), BF16), pltpu.VMEM((tm, LAT_W), F32),
                        pltpu.VMEM((N_HEADS, tm, HEAD), F32)],
        compiler_params=_cparams("parallel", "arbitrary"),
        name="in_proj",
    )(x, g, w, gd, c, s)


def _mla_prep_body(lat_ref, gcq_ref, gckv_ref, wuq_ref, wukv_ref, gqn_ref, gqr_ref, gkn_ref, gkr_ref,
                   c_ref, s_ref, q_ref, k_ref, v_ref):
    c = c_ref[...]
    s = s_ref[...]
    cq = _rmsnorm(lat_ref[:, :Q_LORA], gcq_ref[...]).astype(BF16)
    ckv = _rmsnorm(lat_ref[:, Q_LORA:Q_LORA + KV_LORA], gckv_ref[...]).astype(BF16)
    qa = _dot(cq, wuq_ref[...])
    kva = _dot(ckv, wukv_ref[...])
    kr = lat_ref[:, Q_LORA + KV_LORA:Q_LORA + KV_LORA + LANES]
    kr_ss = jnp.sum(kr * kr, axis=-1, keepdims=True)
    kr_rot = _rope_tile(kr * gkr_ref[...], c, s)
    for h in range(N_HEADS):
        lo = h * MLA_PAD
        qn = qa[:, lo:lo + HEAD]
        qr = qa[:, lo + HEAD:lo + MLA_PAD]
        ss = jnp.sum(qn * qn, axis=-1, keepdims=True) + jnp.sum(qr * qr, axis=-1, keepdims=True)
        r = lax.rsqrt(ss * (1.0 / MLA_QK) + EPS)
        q_ref[:, lo:lo + HEAD] = (qn * r * gqn_ref[...]).astype(BF16)
        q_ref[:, lo + HEAD:lo + MLA_PAD] = (_rope_tile(qr * gqr_ref[...], c, s) * r).astype(BF16)
        kn = kva[:, h * HEAD:(h + 1) * HEAD]
        ssk = jnp.sum(kn * kn, axis=-1, keepdims=True) + kr_ss
        rk = lax.rsqrt(ssk * (1.0 / MLA_QK) + EPS)
        k_ref[:, lo:lo + HEAD] = (kn * rk * gkn_ref[...]).astype(BF16)
        k_ref[:, lo + HEAD:lo + MLA_PAD] = (kr_rot * rk).astype(BF16)
    v_ref[...] = kva[:, HW:].astype(BF16)


def _mla_prep(lat, gcq, gckv, wuq, wukv, gqn, gqr, gkn, gkr, c, s, *, tm):
    T = lat.shape[0]
    row = lambda w: pl.BlockSpec((tm, w), lambda i: (i, 0))
    full = lambda a: pl.BlockSpec(a.shape, lambda i: (0, 0))
    return pl.pallas_call(
        _mla_prep_body,
        grid=(T // tm,),
        in_specs=[row(LAT_W), full(gcq), full(gckv), full(wuq), full(wukv), full(gqn), full(gqr), full(gkn),
                  full(gkr), row(LANES), row(LANES)],
        out_specs=[row(N_HEADS * MLA_PAD), row(N_HEADS * MLA_PAD), row(HW)],
        out_shape=[
            jax.ShapeDtypeStruct((T, N_HEADS * MLA_PAD), BF16),
            jax.ShapeDtypeStruct((T, N_HEADS * MLA_PAD), BF16),
            jax.ShapeDtypeStruct((T, HW), BF16),
        ],
        compiler_params=_cparams("parallel"),
        name="mla_prep",
    )(lat, gcq, gckv, wuq, wukv, gqn, gqr, gkn, gkr, c, s)


def _mla_attn_body(q_ref, k_ref, v_ref, o_ref, *, tq, tk):
    i = pl.program_id(2)
    q = q_ref[...]

    def update(kstart, carry, diag_offset):
        m, l, acc = carry
        s = _dot_nt(q, k_ref[pl.ds(kstart, tk), :])
        if diag_offset is not None:
            row = lax.broadcasted_iota(jnp.int32, s.shape, 0)
            col = lax.broadcasted_iota(jnp.int32, s.shape, 1) + diag_offset
            s = jnp.where(col <= row, s, NEG)
        m_new = jnp.maximum(m, jnp.max(s, axis=-1, keepdims=True))
        alpha = jnp.exp2(m - m_new)
        p = jnp.exp2(s - m_new)
        l = alpha * l + jnp.sum(p, axis=-1, keepdims=True)
        acc = alpha * acc + _dot(p.astype(BF16), v_ref[pl.ds(kstart, tk), :])
        return m_new, l, acc

    init = (jnp.full((tq, 1), NEG, F32), jnp.zeros((tq, 1), F32), jnp.zeros((tq, HEAD), F32))
    carry = lax.fori_loop(0, i * (tq // tk), lambda kc, c: update(pl.multiple_of(kc * tk, tk), c, None), init)
    for c in range(tq // tk):
        carry = update(pl.multiple_of(i * tq + c * tk, tk), carry, c * tk)
    _, l, acc = carry
    o_ref[...] = (acc / l).astype(BF16)


def _mla_attn(q, k, v, *, B, S, tq, tk):
    T = B * S
    nq = S // tq
    return pl.pallas_call(
        functools.partial(_mla_attn_body, tq=tq, tk=tk),
        grid=(B, N_HEADS, nq),
        in_specs=[
            pl.BlockSpec((tq, MLA_PAD), lambda b, h, i: (b * nq + i, h)),
            pl.BlockSpec((S, MLA_PAD), lambda b, h, i: (b, h)),
            pl.BlockSpec((S, HEAD), lambda b, h, i: (b, h)),
        ],
        out_specs=pl.BlockSpec((tq, HEAD), lambda b, h, i: (b * nq + i, h)),
        out_shape=jax.ShapeDtypeStruct((T, HW), BF16),
        compiler_params=_cparams("parallel", "parallel", "arbitrary"),
        name="mla_attn",
    )(q, k, v)


def _band_body(q_ref, k_ref, kh_ref, v_ref, vh_ref, o_ref, st_ref, o32_ref, *, dil, nsub):
    rb = pl.program_id(1)
    h = pl.program_id(2)
    row = lax.broadcasted_iota(jnp.int32, (BAND, BAND), 0)
    col = lax.broadcasted_iota(jnp.int32, (BAND, BAND), 1)
    bias_prev = jnp.where(col >= row, 0.0, NEG).astype(F32)
    bias_cur = jnp.where(col <= row, 0.0, NEG).astype(F32)
    lane = lax.broadcasted_iota(jnp.int32, (BAND, LANES), 1)
    is_m = lane == h
    is_l = lane == h + N_HEADS

    @pl.when(h == 0)
    def _():
        st_ref[...] = jnp.ones_like(st_ref)

    nb = dil * nsub

    def blocks(ref):
        return ref[...].reshape(nb, BAND, HEAD)

    def prev_blocks(ref, halo_ref):
        if nsub == 1:
            return halo_ref[...]
        shifted = jnp.concatenate([halo_ref[...], ref[:, :(nsub - 1) * BAND, :]], axis=1)
        return shifted.reshape(nb, BAND, HEAD)

    bdot_nt = lambda a, b: lax.dot_general(a, b, (((2,), (2,)), ((0,), (0,))), preferred_element_type=F32)
    bdot = lambda a, b: lax.dot_general(a, b, (((2,), (1,)), ((0,), (0,))), preferred_element_type=F32)
    q = blocks(q_ref)
    blk = lax.broadcasted_iota(jnp.int32, (nb, BAND, BAND), 0)
    first = jnp.logical_and(blk % nsub == 0, rb == 0)
    s_prev = bdot_nt(q, prev_blocks(k_ref, kh_ref)) + jnp.where(first, NEG, bias_prev[None])
    s_cur = bdot_nt(q, blocks(k_ref)) + bias_cur[None]
    m = jnp.max(jnp.maximum(s_prev, s_cur), axis=-1, keepdims=True)
    e_prev = jnp.exp2(s_prev - m)
    e_cur = jnp.exp2(s_cur - m)
    l = jnp.sum(e_prev + e_cur, axis=-1, keepdims=True)
    o = bdot(e_prev.astype(BF16), prev_blocks(v_ref, vh_ref)) + bdot(e_cur.astype(BF16), blocks(v_ref))
    o = o / l
    if dil == 1:
        o_ref[...] = o.reshape(nb * BAND, HEAD).astype(BF16)
    for b in range(nb):
        r, c = divmod(b, nsub)
        tok = pl.ds(c * BAND * dil + r, BAND, stride=dil) if dil > 1 else slice(c * BAND, (c + 1) * BAND)
        if dil > 1:
            o32_ref[tok, :] = o[b]
        st_ref[tok, :] = jnp.where(is_m, m[b], jnp.where(is_l, l[b], st_ref[tok, :]))
    if dil > 1:
        o_ref[...] = o32_ref[...].astype(BF16)


def _band_attn(slabs, g, *, B, S):
    dil = DIL_GROUPS[g][1]
    L = S // dil
    chunk = min(S, 2048)
    rb_rows = chunk // dil
    nsub = rb_rows // BAND
    nrb = L // rb_rows
    blk = lambda slab: pl.BlockSpec((None, None, dil, rb_rows, HEAD), lambda b, rb, h: (slab, b, 0, rb, h))
    halo = lambda slab: pl.BlockSpec(
        (None, None, dil, BAND, HEAD), lambda b, rb, h: (slab, b, 0, jnp.maximum(rb * nsub - 1, 0), h))
    T = B * S
    return pl.pallas_call(
        functools.partial(_band_body, dil=dil, nsub=nsub),
        grid=(B, nrb, N_HEADS),
        in_specs=[blk(0), blk(1), halo(1), blk(2), halo(2)],
        out_specs=[
            pl.BlockSpec((chunk, HEAD), lambda b, rb, h: (b * nrb + rb, h)),
            pl.BlockSpec((chunk, LANES), lambda b, rb, h: (b * nrb + rb, 0)),
        ],
        out_shape=[
            jax.ShapeDtypeStruct((T, HW), BF16),
            jax.ShapeDtypeStruct((T, LANES), F32),
        ],
        scratch_shapes=[pltpu.VMEM((chunk, HEAD), F32)],
        compiler_params=_cparams("parallel", "parallel", "arbitrary"),
        name=f"band_attn_{g}",
    )(slabs, slabs, slabs, slabs, slabs)


def _merge_body(x_ref, om_ref, o0_ref, o1_ref, o2_ref, st0_ref, st1_ref, st2_ref,
                ga_ref, gb_ref, wa_ref, wb_ref, wo_ref, out_ref, od_ref):
    f = pl.program_id(1)

    @pl.when(f == 0)
    def _():
        m0, m1, m2 = st0_ref[...], st1_ref[...], st2_ref[...]
        l0, l1, l2 = (pltpu.roll(st, LANES - N_HEADS, 1) for st in (m0, m1, m2))
        mx = jnp.maximum(jnp.maximum(m0, m1), m2)
        w0 = l0 * jnp.exp2(m0 - mx)
        w1 = l1 * jnp.exp2(m1 - mx)
        w2 = l2 * jnp.exp2(m2 - mx)
        tot = w0 + w1 + w2
        w0, w1, w2 = w0 / tot, w1 / tot, w2 / tot
        tm = w0.shape[0]
        for h in range(N_HEADS):
            hs = slice(h * HEAD, (h + 1) * HEAD)
            bc = lambda w: jnp.broadcast_to(w[:, h:h + 1], (tm, HEAD))
            od = (bc(w0) * o0_ref[:, hs].astype(F32) + bc(w1) * o1_ref[:, hs].astype(F32)
                  + bc(w2) * o2_ref[:, hs].astype(F32))
            od_ref[:, hs] = od.astype(BF16)
        out_ref[...] = x_ref[...]

    a = _dot(om_ref[...], wa_ref[...])
    b = _dot(od_ref[...], wb_ref[...])
    merged = jax.nn.sigmoid(ga_ref[...].astype(F32)) * a + jax.nn.sigmoid(gb_ref[...].astype(F32)) * b
    out_ref[...] += _dot(merged.astype(BF16), wo_ref[...])


def _merge(x, o_mla, o_dil, st_dil, gates, wa, wb, wo, *, tm, tf):
    T, D = x.shape
    nf = D // tf
    row = lambda w: pl.BlockSpec((tm, w), lambda i, f: (i, 0))
    return pl.pallas_call(
        _merge_body,
        grid=(T // tm, nf),
        in_specs=[row(D), row(HW), row(HW), row(HW), row(HW)] + [row(LANES)] * 3 + [
            pl.BlockSpec((tm, tf), lambda i, f: (i, f)),
            pl.BlockSpec((tm, tf), lambda i, f: (i, nf + f)),
            pl.BlockSpec((HW, tf), lambda i, f: (0, f)),
            pl.BlockSpec((HW, tf), lambda i, f: (0, f)),
            pl.BlockSpec((tf, D), lambda i, f: (f, 0)),
        ],
        out_specs=row(D),
        out_shape=jax.ShapeDtypeStruct((T, D), F32),
        scratch_shapes=[pltpu.VMEM((tm, HW), BF16)],
        compiler_params=_cparams("parallel", "arbitrary"),
        name="merge",
    )(x, o_mla, *o_dil, *st_dil, gates, gates, wa, wb, wo)


def _ple_body(x_ref, g_ref, p_ref, wg_ref, wp_ref, o_ref):
    x = x_ref[...]
    h = _rmsnorm(x, g_ref[...]).astype(BF16)
    gate = jax.nn.sigmoid(_dot(h, wg_ref[...]))
    o_ref[...] = x + gate * _dot(p_ref[...].astype(BF16), wp_ref[...])


def _ple(x, g, p, wg, wp, *, tm):
    T, D = x.shape
    P = p.shape[1]
    return pl.pallas_call(
        _ple_body,
        grid=(T // tm,),
        in_specs=[
            pl.BlockSpec((tm, D), lambda i: (i, 0)),
            pl.BlockSpec((1, D), lambda i: (0, 0)),
            pl.BlockSpec((tm, P), lambda i: (i, 0)),
            pl.BlockSpec((D, D), lambda i: (0, 0)),
            pl.BlockSpec((P, D), lambda i: (0, 0)),
        ],
        out_specs=pl.BlockSpec((tm, D), lambda i: (i, 0)),
        out_shape=jax.ShapeDtypeStruct((T, D), F32),
        compiler_params=_cparams("parallel"),
        name="ple",
    )(x, g, p, wg, wp)


def _spread_rope(a):
    half = MLA_ROPE // 2
    z = jnp.zeros(a.shape[:-1] + (LANES // 2 - half,), a.dtype)
    return jnp.concatenate([a[..., :half], z, a[..., half:], z], axis=-1)


def _layer(x, p, i, g_ffn1, w1_gate, w1_up, w1_down, g_mix, w_in, g_cq, w_uq, g_ckv, w_ukv, g_q_mla, g_k_mla,
           g_q_dil, g_k_dil, w_br_mla, w_br_dil, w_o, g_ffn2, w2_gate, w2_up, w2_down, g_ple, w_ple_gate,
           w_ple_proj, tables, B, S):
    T, D = x.shape
    tm = min(T, 512)
    tf = min(w1_gate.shape[-1], 512)
    bf = lambda a: a.astype(BF16)
    row = lambda a: a.reshape(1, -1).astype(F32)
    (c_mla, s_mla), (c_dil, s_dil) = tables
    log2e = 1.4426950408889634

    x = _ffn(x, row(g_ffn1[i]), bf(w1_gate[i]), bf(w1_up[i]), bf(w1_down[i]), tm=min(T, 1024), tf=tf)

    off_kr = Q_LORA + KV_LORA
    off_dil = off_kr + MLA_ROPE
    off_gate = off_dil + 3 * N_GROUPS * HW
    w = bf(w_in[i])
    pieces = [w[:, :off_kr], _spread_rope(w[:, off_kr:off_dil]), jnp.zeros((D, LAT_W - off_kr - LANES), BF16)]
    for slab in range(3 * N_GROUPS):
        base = off_dil + slab * HW
        if slab % 3 == 2:
            pieces.append(w[:, base:base + HW])
        else:
            pieces += [_split_pair(w[:, base + h * HEAD:base + (h + 1) * HEAD], DIL_ROT // 2)
                       for h in range(N_HEADS)]
    w_r = jnp.concatenate(pieces + [w[:, off_gate:]], axis=1)
    scale_dil = HEAD ** -0.5 * log2e
    pair = lambda v: _split_pair(v, DIL_ROT // 2)
    tile_heads = lambda v: jnp.tile(v, N_HEADS).reshape(1, HW)
    gd = jnp.stack([tile_heads(t) for g in range(N_GROUPS)
                    for t in (pair(g_q_dil[i, g]) * scale_dil, pair(g_k_dil[i, g]), jnp.ones((HEAD,), F32))],
                   axis=0)
    lat, d0, d1, d2, gates = _in_proj(x, row(g_mix[i]), w_r, gd, c_dil, s_dil, tm=tm, B=B, S=S)

    scale_mla = MLA_QK ** -0.5 * log2e
    wuq = w_uq[i].reshape(Q_LORA, N_HEADS, MLA_QK)
    wuq = jnp.concatenate([wuq[..., :HEAD], _spread_rope(wuq[..., HEAD:])], axis=-1)
    wuq = bf(wuq.reshape(Q_LORA, N_HEADS * MLA_PAD))
    wukv = bf(w_ukv[i].reshape(KV_LORA, N_HEADS, 2, HEAD).transpose(0, 2, 1, 3).reshape(KV_LORA, 2 * HW))
    gq, gk = g_q_mla[i] * scale_mla, g_k_mla[i]
    q, k, v = _mla_prep(lat, row(g_cq[i]), row(g_ckv[i]), wuq, wukv, row(gq[:HEAD]), row(_spread_rope(gq[HEAD:])),
                        row(gk[:HEAD]), row(_spread_rope(gk[HEAD:])), c_mla, s_mla, tm=tm)
    o_mla = _mla_attn(q, k, v, B=B, S=S, tq=min(S, 1024), tk=min(S, 1024))

    outs = [_band_attn(d, g, B=B, S=S) for g, d in enumerate((d0, d1, d2))]
    o_dil, st_dil = zip(*outs)

    x = _merge(x, o_mla, o_dil, st_dil, gates, bf(w_br_mla[i]), bf(w_br_dil[i]), bf(w_o[i]),
               tm=tm, tf=min(D, 1024))
    x = _ffn(x, row(g_ffn2[i]), bf(w2_gate[i]), bf(w2_up[i]), bf(w2_down[i]), tm=tm, tf=tf)
    x = _ple(x, row(g_ple[i]), p[i].reshape(T, -1), bf(w_ple_gate[i]), bf(w_ple_proj[i]), tm=tm)
    return x


def kernel(x, p, positions, g_ffn1, w1_gate, w1_up, w1_down, g_mix, w_in, g_cq, w_uq, g_ckv, w_ukv, g_q_mla,
           g_k_mla, g_q_dil, g_k_dil, w_br_mla, w_br_dil, w_o, g_ffn2, w2_gate, w2_up, w2_down, g_ple,
           w_ple_gate, w_ple_proj):
    B, S, D = x.shape
    T = B * S
    pos = positions.reshape(T)
    tables = _rope_tables(pos)
    xt = x.reshape(T, D)
    for i in range(g_ffn1.shape[0]):
        xt = _layer(xt, p, i, g_ffn1, w1_gate, w1_up, w1_down, g_mix, w_in, g_cq, w_uq, g_ckv, w_ukv,
                    g_q_mla, g_k_mla, g_q_dil, g_k_dil, w_br_mla, w_br_dil, w_o, g_ffn2, w2_gate, w2_up,
                    w2_down, g_ple, w_ple_gate, w_ple_proj, tables, B, S)
    return xt.reshape(B, S, D)
```

```python
import functools

import jax
import jax.numpy as jnp
from jax import lax
from jax.experimental import pallas as pl
from jax.experimental.pallas import tpu as pltpu

F32 = jnp.float32
BF16 = jnp.bfloat16

EPS = 1e-6
NEG = -1e30
ROPE_THETA = 500000.0

LANES = 128
N_HEADS = 8
HEAD = 128
MLA_ROPE = 64
MLA_QK = HEAD + MLA_ROPE
MLA_PAD = 256
Q_LORA = 512
KV_LORA = 256
LAT_W = 1024
DIL_GROUPS = ((128, 1), (512, 4), (2048, 16))
N_GROUPS = len(DIL_GROUPS)
DIL_ROT = 32
BAND = 128
HW = N_HEADS * HEAD

VMEM_LIMIT = 56 * 1024 * 1024


def _cparams(*sem):
    return pltpu.CompilerParams(dimension_semantics=sem, vmem_limit_bytes=VMEM_LIMIT)


def _rmsnorm(x, g):
    ms = jnp.mean(x * x, axis=-1, keepdims=True)
    return x * lax.rsqrt(ms + EPS) * g


def _dot(a, b):
    return jnp.dot(a, b, preferred_element_type=F32)


def _dot_nt(a, b):
    return lax.dot_general(a, b, (((1,), (1,)), ((), ())), preferred_element_type=F32)


def _rope_tile(y, c, s):
    return y * c + pltpu.roll(y, LANES // 2, 1) * s


def _trig_body(pos_ref, inv_ref, cm_ref, sm_ref, cd_ref, sd_ref):
    hd, hm, mid = DIL_ROT // 2, MLA_ROPE // 2, LANES // 2
    ang = pos_ref[...].astype(F32) * inv_ref[...]
    cosv = jnp.cos(ang)
    sinv = jnp.sin(ang)
    lane = lax.broadcasted_iota(jnp.int32, ang.shape, 1)

    def tables(cv, sv, half):
        x1 = lane < half
        x2 = jnp.logical_and(lane >= mid, lane < mid + half)
        c = jnp.where(x1, cv, jnp.where(x2, cv, 1.0))
        s = jnp.where(x1, -sv, jnp.where(x2, sv, 0.0))
        return c, s

    cd_ref[...], sd_ref[...] = tables(cosv, sinv, hd)
    cm_ref[...], sm_ref[...] = tables(pltpu.roll(cosv, LANES - hd, 1), pltpu.roll(sinv, LANES - hd, 1), hm)


def _rope_tables(pos):
    T = pos.shape[0]
    hd, hm, mid = DIL_ROT // 2, MLA_ROPE // 2, LANES // 2
    inv_d = ROPE_THETA ** (-jnp.arange(hd, dtype=F32) * 2.0 / (2 * hd))
    inv_m = ROPE_THETA ** (-jnp.arange(hm, dtype=F32) * 2.0 / (2 * hm))
    pad = jnp.zeros((mid - hd - hm,), F32)
    inv = jnp.concatenate([inv_d, inv_m, pad, inv_d, inv_m, pad]).reshape(1, LANES)
    tr = min(T, 512)
    tab = pl.BlockSpec((tr, LANES), lambda i: (i, 0))
    cm, sm, cd, sd = pl.pallas_call(
        _trig_body,
        grid=(T // tr,),
        in_specs=[pl.BlockSpec((tr, 1), lambda i: (i, 0)), pl.BlockSpec((1, LANES), lambda i: (0, 0))],
        out_specs=[tab] * 4,
        out_shape=[jax.ShapeDtypeStruct((T, LANES), F32)] * 4,
        compiler_params=_cparams("parallel"),
        name="rope_trig",
    )(pos.reshape(T, 1), inv)
    return (cm, sm), (cd, sd)


def _split_pair(a, half):
    n = LANES // 2 - half
    return jnp.concatenate([a[..., :half], a[..., 2 * half:2 * half + n], a[..., half:2 * half],
                            a[..., 2 * half + n:]], axis=-1)


def _ffn_body(x_ref, g_ref, wg_ref, wu_ref, wd_ref, o_ref, h_ref):
    f = pl.program_id(1)

    @pl.when(f == 0)
    def _():
        x = x_ref[...]
        h_ref[...] = _rmsnorm(x, g_ref[...]).astype(BF16)
        o_ref[...] = x

    h = h_ref[...]
    a = _dot(h, wg_ref[...])
    u = _dot(h, wu_ref[...])
    act = (0.5 * a) * jax.nn.sigmoid(a) * u
    o_ref[...] += _dot(act.astype(BF16), wd_ref[...])


def _ffn(x, g, wg, wu, wd, *, tm, tf):
    T, D = x.shape
    F = wg.shape[1]
    return pl.pallas_call(
        _ffn_body,
        grid=(T // tm, F // tf),
        in_specs=[
            pl.BlockSpec((tm, D), lambda i, f: (i, 0)),
            pl.BlockSpec((1, D), lambda i, f: (0, 0)),
            pl.BlockSpec((D, tf), lambda i, f: (0, f)),
            pl.BlockSpec((D, tf), lambda i, f: (0, f)),
            pl.BlockSpec((tf, D), lambda i, f: (f, 0)),
        ],
        out_specs=pl.BlockSpec((tm, D), lambda i, f: (i, 0)),
        out_shape=jax.ShapeDtypeStruct((T, D), F32),
        scratch_shapes=[pltpu.VMEM((tm, D), BF16)],
        compiler_params=_cparams("parallel", "arbitrary"),
        name="ffn",
    )(x, g, wg, wu, wd)


def _proj_body(x_ref, g_ref, w_ref, gd_ref, c_ref, s_ref, lat_ref, d0_ref, d1_ref, d2_ref, gate_ref,
               h_ref, y_ref, z_ref, *, n_dil):
    j = pl.program_id(1)
    tm = y_ref.shape[0]

    @pl.when(j == 0)
    def _():
        h_ref[...] = _rmsnorm(x_ref[...], g_ref[...]).astype(BF16)

    y_ref[...] = _dot(h_ref[...], w_ref[...])

    @pl.when(j == 0)
    def _():
        lat_ref[...] = y_ref[...]

    is_dil = jnp.logical_and(j >= 1, j <= n_dil)
    is_v = (j - 1) % 3 == 2

    @pl.when(jnp.logical_and(is_dil, is_v))
    def _():
        for h in range(N_HEADS):
            z_ref[h] = y_ref[:, h * HEAD:(h + 1) * HEAD]

    @pl.when(jnp.logical_and(is_dil, jnp.logical_not(is_v)))
    def _():
        c = c_ref[...]
        s = s_ref[...]
        pair = 2 * HEAD
        same_head = (lax.broadcasted_iota(jnp.int32, (pair, pair), 0) // HEAD
                     == lax.broadcasted_iota(jnp.int32, (pair, pair), 1) // HEAD)
        ones_bd = jnp.where(same_head, 1.0, 0.0).astype(BF16)
        for hp in range(N_HEADS // 2):
            sl = slice(hp * pair, (hp + 1) * pair)
            y = y_ref[:, sl]
            ss = _dot((y * y).astype(BF16), ones_bd)
            yn = y * lax.rsqrt(ss * (1.0 / HEAD) + EPS) * gd_ref[:, sl]
            for t in range(2):
                z_ref[2 * hp + t] = _rope_tile(yn[:, t * HEAD:(t + 1) * HEAD], c, s)

    for g, d_ref in enumerate((d0_ref, d1_ref, d2_ref)):
        dil = DIL_GROUPS[g][1]

        @pl.when(jnp.logical_and(j >= 1 + 3 * g, j <= 3 + 3 * g))
        def _(d_ref=d_ref, dil=dil):
            for h in range(N_HEADS):
                for r in range(dil):
                    rows = pl.ds(r, tm // dil, stride=dil) if dil > 1 else slice(None)
                    d_ref[r, :, h * HEAD:(h + 1) * HEAD] = z_ref[h, rows, :].astype(BF16)

    @pl.when(j > n_dil)
    def _():
        gate_ref[...] = y_ref[...].astype(BF16)


def _in_proj(x, g, w, gd, c, s, *, tm, B, S):
    T, D = x.shape
    n_dil = 3 * N_GROUPS
    n_gate = (2 * D) // LAT_W
    nj = 1 + n_dil + n_gate
    assert w.shape[1] == nj * LAT_W
    per_b = S // tm

    def dil_spec(g):
        dil = DIL_GROUPS[g][1]
        return pl.BlockSpec((None, None, dil, tm // dil, HW),
                            lambda i, j: (jnp.clip(j - 1 - 3 * g, 0, 2), i // per_b, 0, i % per_b, 0))

    def dil_shape(g):
        dil = DIL_GROUPS[g][1]
        return jax.ShapeDtypeStruct((3, B, dil, S // dil, HW), BF16)

    tab = pl.BlockSpec((tm, LANES), lambda i, j: (i, 0))
    return pl.pallas_call(
        functools.partial(_proj_body, n_dil=n_dil),
        grid=(T // tm, nj),
        in_specs=[
            pl.BlockSpec((tm, D), lambda i, j: (i, 0)),
            pl.BlockSpec((1, D), lambda i, j: (0, 0)),
            pl.BlockSpec((D, LAT_W), lambda i, j: (0, j)),
            pl.BlockSpec((None, 1, HW), lambda i, j: (jnp.clip(j - 1, 0, n_dil - 1), 0, 0)),
            tab, tab,
        ],
        out_specs=[
            pl.BlockSpec((tm, LAT_W), lambda i, j: (i, 0)),
            dil_spec(0), dil_spec(1), dil_spec(2),
            pl.BlockSpec((tm, LAT_W), lambda i, j: (i, jnp.clip(j - 1 - n_dil, 0, n_gate - 1))),
        ],
        out_shape=[
            jax.ShapeDtypeStruct((T, LAT_W), F32),
            dil_shape(0), dil_shape(1), dil_shape(2),
            jax.ShapeDtypeStruct((T, 2 * D), BF16),
        ],
        scratch_shapes=[pltpu.VMEM((tm, D), BF16), pltpu.VMEM((tm, LAT_W), F32),
                        pltpu.VMEM((N_HEADS, tm, HEAD), F32)],
        compiler_params=_cparams("parallel", "arbitrary"),
        name="in_proj",
    )(x, g, w, gd, c, s)


def _mla_prep_body(lat_ref, gcq_ref, gckv_ref, wuq_ref, wukv_ref, gqn_ref, gqr_ref, gkn_ref, gkr_ref,
                   c_ref, s_ref, q_ref, k_ref, v_ref):
    c = c_ref[...]
    s = s_ref[...]
    cq = _rmsnorm(lat_ref[:, :Q_LORA], gcq_ref[...]).astype(BF16)
    ckv = _rmsnorm(lat_ref[:, Q_LORA:Q_LORA + KV_LORA], gckv_ref[...]).astype(BF16)
    qa = _dot(cq, wuq_ref[...])
    kva = _dot(ckv, wukv_ref[...])
    kr = lat_ref[:, Q_LORA + KV_LORA:Q_LORA + KV_LORA + LANES]
    kr_ss = jnp.sum(kr * kr, axis=-1, keepdims=True)
    kr_rot = _rope_tile(kr * gkr_ref[...], c, s)
    for h in range(N_HEADS):
        lo = h * MLA_PAD
        qn = qa[:, lo:lo + HEAD]
        qr = qa[:, lo + HEAD:lo + MLA_PAD]
        ss = jnp.sum(qn * qn, axis=-1, keepdims=True) + jnp.sum(qr * qr, axis=-1, keepdims=True)
        r = lax.rsqrt(ss * (1.0 / MLA_QK) + EPS)
        q_ref[:, lo:lo + HEAD] = (qn * r * gqn_ref[...]).astype(BF16)
        q_ref[:, lo + HEAD:lo + MLA_PAD] = (_rope_tile(qr * gqr_ref[...], c, s) * r).astype(BF16)
        kn = kva[:, h * HEAD:(h + 1) * HEAD]
        ssk = jnp.sum(kn * kn, axis=-1, keepdims=True) + kr_ss
        rk = lax.rsqrt(ssk * (1.0 / MLA_QK) + EPS)
        k_ref[:, lo:lo + HEAD] = (kn * rk * gkn_ref[...]).astype(BF16)
        k_ref[:, lo + HEAD:lo + MLA_PAD] = (kr_rot * rk).astype(BF16)
    v_ref[...] = kva[:, HW:].astype(BF16)


def _mla_prep(lat, gcq, gckv, wuq, wukv, gqn, gqr, gkn, gkr, c, s, *, tm):
    T = lat.shape[0]
    row = lambda w: pl.BlockSpec((tm, w), lambda i: (i, 0))
    full = lambda a: pl.BlockSpec(a.shape, lambda i: (0, 0))
    return pl.pallas_call(
        _mla_prep_body,
        grid=(T // tm,),
        in_specs=[row(LAT_W), full(gcq), full(gckv), full(wuq), full(wukv), full(gqn), full(gqr), full(gkn),
                  full(gkr), row(LANES), row(LANES)],
        out_specs=[row(N_HEADS * MLA_PAD), row(N_HEADS * MLA_PAD), row(HW)],
        out_shape=[
            jax.ShapeDtypeStruct((T, N_HEADS * MLA_PAD), BF16),
            jax.ShapeDtypeStruct((T, N_HEADS * MLA_PAD), BF16),
            jax.ShapeDtypeStruct((T, HW), BF16),
        ],
        compiler_params=_cparams("parallel"),
        name="mla_prep",
    )(lat, gcq, gckv, wuq, wukv, gqn, gqr, gkn, gkr, c, s)


def _mla_attn_body(q_ref, k_ref, v_ref, o_ref, *, tq, tk):
    i = pl.program_id(2)
    q = q_ref[...]

    def update(kstart, carry, diag_offset):
        m, l, acc = carry
        s = _dot_nt(q, k_ref[pl.ds(kstart, tk), :])
        if diag_offset is not None:
            row = lax.broadcasted_iota(jnp.int32, s.shape, 0)
            col = lax.broadcasted_iota(jnp.int32, s.shape, 1) + diag_offset
            s = jnp.where(col <= row, s, NEG)
        m_new = jnp.maximum(m, jnp.max(s, axis=-1, keepdims=True))
        alpha = jnp.exp2(m - m_new)
        p = jnp.exp2(s - m_new)
        l = alpha * l + jnp.sum(p, axis=-1, keepdims=True)
        acc = alpha * acc + _dot(p.astype(BF16), v_ref[pl.ds(kstart, tk), :])
        return m_new, l, acc

    init = (jnp.full((tq, 1), NEG, F32), jnp.zeros((tq, 1), F32), jnp.zeros((tq, HEAD), F32))
    carry = lax.fori_loop(0, i * (tq // tk), lambda kc, c: update(pl.multiple_of(kc * tk, tk), c, None), init)
    for c in range(tq // tk):
        carry = update(pl.multiple_of(i * tq + c * tk, tk), carry, c * tk)
    _, l, acc = carry
    o_ref[...] = (acc / l).astype(BF16)


def _mla_attn(q, k, v, *, B, S, tq, tk):
    T = B * S
    nq = S // tq
    return pl.pallas_call(
        functools.partial(_mla_attn_body, tq=tq, tk=tk),
        grid=(B, N_HEADS, nq),
        in_specs=[
            pl.BlockSpec((tq, MLA_PAD), lambda b, h, i: (b * nq + i, h)),
            pl.BlockSpec((S, MLA_PAD), lambda b, h, i: (b, h)),
            pl.BlockSpec((S, HEAD), lambda b, h, i: (b, h)),
        ],
        out_specs=pl.BlockSpec((tq, HEAD), lambda b, h, i: (b * nq + i, h)),
        out_shape=jax.ShapeDtypeStruct((T, HW), BF16),
        compiler_params=_cparams("parallel", "parallel", "arbitrary"),
        name="mla_attn",
    )(q, k, v)


def _band_body(q_ref, k_ref, kh_ref, v_ref, vh_ref, o_ref, st_ref, o32_ref, *, dil, nsub):
    rb = pl.program_id(1)
    h = pl.program_id(2)
    row = lax.broadcasted_iota(jnp.int32, (BAND, BAND), 0)
    col = lax.broadcasted_iota(jnp.int32, (BAND, BAND), 1)
    bias_prev = jnp.where(col >= row, 0.0, NEG).astype(F32)
    bias_cur = jnp.where(col <= row, 0.0, NEG).astype(F32)
    lane = lax.broadcasted_iota(jnp.int32, (BAND, LANES), 1)
    is_m = lane == h
    is_l = lane == h + N_HEADS

    @pl.when(h == 0)
    def _():
        st_ref[...] = jnp.ones_like(st_ref)

    nb = dil * nsub

    def blocks(ref):
        return ref[...].reshape(nb, BAND, HEAD)

    def prev_blocks(ref, halo_ref):
        if nsub == 1:
            return halo_ref[...]
        shifted = jnp.concatenate([halo_ref[...], ref[:, :(nsub - 1) * BAND, :]], axis=1)
        return shifted.reshape(nb, BAND, HEAD)

    bdot_nt = lambda a, b: lax.dot_general(a, b, (((2,), (2,)), ((0,), (0,))), preferred_element_type=F32)
    bdot = lambda a, b: lax.dot_general(a, b, (((2,), (1,)), ((0,), (0,))), preferred_element_type=F32)
    q = blocks(q_ref)
    blk = lax.broadcasted_iota(jnp.int32, (nb, BAND, BAND), 0)
    first = jnp.logical_and(blk % nsub == 0, rb == 0)
    s_prev = bdot_nt(q, prev_blocks(k_ref, kh_ref)) + jnp.where(first, NEG, bias_prev[None])
    s_cur = bdot_nt(q, blocks(k_ref)) + bias_cur[None]
    m = jnp.max(jnp.maximum(s_prev, s_cur), axis=-1, keepdims=True)
    e_prev = jnp.exp2(s_prev - m)
    e_cur = jnp.exp2(s_cur - m)
    l = jnp.sum(e_prev + e_cur, axis=-1, keepdims=True)
    o = bdot(e_prev.astype(BF16), prev_blocks(v_ref, vh_ref)) + bdot(e_cur.astype(BF16), blocks(v_ref))
    o = o / l
    if dil == 1:
        o_ref[...] = o.reshape(nb * BAND, HEAD).astype(BF16)
    for b in range(nb):
        r, c = divmod(b, nsub)
        tok = pl.ds(c * BAND * dil + r, BAND, stride=dil) if dil > 1 else slice(c * BAND, (c + 1) * BAND)
        if dil > 1:
            o32_ref[tok, :] = o[b]
        st_ref[tok, :] = jnp.where(is_m, m[b], jnp.where(is_l, l[b], st_ref[tok, :]))
    if dil > 1:
        o_ref[...] = o32_ref[...].astype(BF16)


def _band_attn(slabs, g, *, B, S):
    dil = DIL_GROUPS[g][1]
    L = S // dil
    chunk = min(S, 2048)
    rb_rows = chunk // dil
    nsub = rb_rows // BAND
    nrb = L // rb_rows
    blk = lambda slab: pl.BlockSpec((None, None, dil, rb_rows, HEAD), lambda b, rb, h: (slab, b, 0, rb, h))
    halo = lambda slab: pl.BlockSpec(
        (None, None, dil, BAND, HEAD), lambda b, rb, h: (slab, b, 0, jnp.maximum(rb * nsub - 1, 0), h))
    T = B * S
    return pl.pallas_call(
        functools.partial(_band_body, dil=dil, nsub=nsub),
        grid=(B, nrb, N_HEADS),
        in_specs=[blk(0), blk(1), halo(1), blk(2), halo(2)],
        out_specs=[
            pl.BlockSpec((chunk, HEAD), lambda b, rb, h: (b * nrb + rb, h)),
            pl.BlockSpec((chunk, LANES), lambda b, rb, h: (b * nrb + rb, 0)),
        ],
        out_shape=[
            jax.ShapeDtypeStruct((T, HW), BF16),
            jax.ShapeDtypeStruct((T, LANES), F32),
        ],
        scratch_shapes=[pltpu.VMEM((chunk, HEAD), F32)],
        compiler_params=_cparams("parallel", "parallel", "arbitrary"),
        name=f"band_attn_{g}",
    )(slabs, slabs, slabs, slabs, slabs)


def _merge_body(x_ref, om_ref, o0_ref, o1_ref, o2_ref, st0_ref, st1_ref, st2_ref,
                ga_ref, gb_ref, wa_ref, wb_ref, wo_ref, out_ref, od_ref):
    f = pl.program_id(1)

    @pl.when(f == 0)
    def _():
        m0, m1, m2 = st0_ref[...], st1_ref[...], st2_ref[...]
        l0, l1, l2 = (pltpu.roll(st, LANES - N_HEADS, 1) for st in (m0, m1, m2))
        mx = jnp.maximum(jnp.maximum(m0, m1), m2)
        w0 = l0 * jnp.exp2(m0 - mx)
        w1 = l1 * jnp.exp2(m1 - mx)
        w2 = l2 * jnp.exp2(m2 - mx)
        tot = w0 + w1 + w2
        w0, w1, w2 = w0 / tot, w1 / tot, w2 / tot
        tm = w0.shape[0]
        for h in range(N_HEADS):
            hs = slice(h * HEAD, (h + 1) * HEAD)
            bc = lambda w: jnp.broadcast_to(w[:, h:h + 1], (tm, HEAD))
            od = (bc(w0) * o0_ref[:, hs].astype(F32) + bc(w1) * o1_ref[:, hs].astype(F32)
                  + bc(w2) * o2_ref[:, hs].astype(F32))
            od_ref[:, hs] = od.astype(BF16)
        out_ref[...] = x_ref[...]

    a = _dot(om_ref[...], wa_ref[...])
    b = _dot(od_ref[...], wb_ref[...])
    merged = jax.nn.sigmoid(ga_ref[...].astype(F32)) * a + jax.nn.sigmoid(gb_ref[...].astype(F32)) * b
    out_ref[...] += _dot(merged.astype(BF16), wo_ref[...])


def _merge(x, o_mla, o_dil, st_dil, gates, wa, wb, wo, *, tm, tf):
    T, D = x.shape
    nf = D // tf
    row = lambda w: pl.BlockSpec((tm, w), lambda i, f: (i, 0))
    return pl.pallas_call(
        _merge_body,
        grid=(T // tm, nf),
        in_specs=[row(D), row(HW), row(HW), row(HW), row(HW)] + [row(LANES)] * 3 + [
            pl.BlockSpec((tm, tf), lambda i, f: (i, f)),
            pl.BlockSpec((tm, tf), lambda i, f: (i, nf + f)),
            pl.BlockSpec((HW, tf), lambda i, f: (0, f)),
            pl.BlockSpec((HW, tf), lambda i, f: (0, f)),
            pl.BlockSpec((tf, D), lambda i, f: (f, 0)),
        ],
        out_specs=row(D),
        out_shape=jax.ShapeDtypeStruct((T, D), F32),
        scratch_shapes=[pltpu.VMEM((tm, HW), BF16)],
        compiler_params=_cparams("parallel", "arbitrary"),
        name="merge",
    )(x, o_mla, *o_dil, *st_dil, gates, gates, wa, wb, wo)


def _ple_body(x_ref, g_ref, p_ref, wg_ref, wp_ref, o_ref):
    x = x_ref[...]
    h = _rmsnorm(x, g_ref[...]).astype(BF16)
    gate = jax.nn.sigmoid(_dot(h, wg_ref[...]))
    o_ref[...] = x + gate * _dot(p_ref[...].astype(BF16), wp_ref[...])


def _ple(x, g, p, wg, wp, *, tm):
    T, D = x.shape
    P = p.shape[1]
    return pl.pallas_call(
        _ple_body,
        grid=(T // tm,),
        in_specs=[
            pl.BlockSpec((tm, D), lambda i: (i, 0)),
            pl.BlockSpec((1, D), lambda i: (0, 0)),
            pl.BlockSpec((tm, P), lambda i: (i, 0)),
            pl.BlockSpec((D, D), lambda i: (0, 0)),
            pl.BlockSpec((P, D), lambda i: (0, 0)),
        ],
        out_specs=pl.BlockSpec((tm, D), lambda i: (i, 0)),
        out_shape=jax.ShapeDtypeStruct((T, D), F32),
        compiler_params=_cparams("parallel"),
        name="ple",
    )(x, g, p, wg, wp)


def _spread_rope(a):
    half = MLA_ROPE // 2
    z = jnp.zeros(a.shape[:-1] + (LANES // 2 - half,), a.dtype)
    return jnp.concatenate([a[..., :half], z, a[..., half:], z], axis=-1)


def _layer(x, p, i, g_ffn1, w1_gate, w1_up, w1_down, g_mix, w_in, g_cq, w_uq, g_ckv, w_ukv, g_q_mla, g_k_mla,
           g_q_dil, g_k_dil, w_br_mla, w_br_dil, w_o, g_ffn2, w2_gate, w2_up, w2_down, g_ple, w_ple_gate,
           w_ple_proj, tables, B, S):
    T, D = x.shape
    tm = min(T, 512)
    tf = min(w1_gate.shape[-1], 512)
    bf = lambda a: a.astype(BF16)
    row = lambda a: a.reshape(1, -1).astype(F32)
    (c_mla, s_mla), (c_dil, s_dil) = tables
    log2e = 1.4426950408889634

    x = _ffn(x, row(g_ffn1[i]), bf(w1_gate[i]), bf(w1_up[i]), bf(w1_down[i]), tm=min(T, 1024), tf=tf)

    off_kr = Q_LORA + KV_LORA
    off_dil = off_kr + MLA_ROPE
    off_gate = off_dil + 3 * N_GROUPS * HW
    w = bf(w_in[i])
    wd = w[:, off_dil:off_gate].reshape(D, N_GROUPS, 3, N_HEADS, HEAD)
    wd_qk = _split_pair(wd[:, :, :2], DIL_ROT // 2)
    wd = jnp.concatenate([wd_qk, wd[:, :, 2:]], axis=2).reshape(D, 3 * N_GROUPS * HW)
    w_r = jnp.concatenate([w[:, :off_kr], _spread_rope(w[:, off_kr:off_dil]),
                           jnp.zeros((D, LAT_W - off_kr - LANES), BF16), wd, w[:, off_gate:]], axis=1)
    scale_dil = HEAD ** -0.5 * log2e
    pair = lambda v: _split_pair(v, DIL_ROT // 2)
    tile_heads = lambda v: jnp.tile(v, N_HEADS).reshape(1, HW)
    gd = jnp.stack([tile_heads(t) for g in range(N_GROUPS)
                    for t in (pair(g_q_dil[i, g]) * scale_dil, pair(g_k_dil[i, g]), jnp.ones((HEAD,), F32))],
                   axis=0)
    lat, d0, d1, d2, gates = _in_proj(x, row(g_mix[i]), w_r, gd, c_dil, s_dil, tm=tm, B=B, S=S)

    scale_mla = MLA_QK ** -0.5 * log2e
    wuq = w_uq[i].reshape(Q_LORA, N_HEADS, MLA_QK)
    wuq = jnp.concatenate([wuq[..., :HEAD], _spread_rope(wuq[..., HEAD:])], axis=-1)
    wuq = bf(wuq.reshape(Q_LORA, N_HEADS * MLA_PAD))
    wukv = bf(w_ukv[i].reshape(KV_LORA, N_HEADS, 2, HEAD).transpose(0, 2, 1, 3).reshape(KV_LORA, 2 * HW))
    gq, gk = g_q_mla[i] * scale_mla, g_k_mla[i]
    q, k, v = _mla_prep(lat, row(g_cq[i]), row(g_ckv[i]), wuq, wukv, row(gq[:HEAD]), row(_spread_rope(gq[HEAD:])),
                        row(gk[:HEAD]), row(_spread_rope(gk[HEAD:])), c_mla, s_mla, tm=tm)
    o_mla = _mla_attn(q, k, v, B=B, S=S, tq=min(S, 1024), tk=min(S, 1024))

    outs = [_band_attn(d, g, B=B, S=S) for g, d in enumerate((d0, d1, d2))]
    o_dil, st_dil = zip(*outs)

    x = _merge(x, o_mla, o_dil, st_dil, gates, bf(w_br_mla[i]), bf(w_br_dil[i]), bf(w_o[i]),
               tm=tm, tf=min(D, 1024))
    x = _ffn(x, row(g_ffn2[i]), bf(w2_gate[i]), bf(w2_up[i]), bf(w2_down[i]), tm=min(T, 1024), tf=tf)
    x = _ple(x, row(g_ple[i]), p[i].reshape(T, -1), bf(w_ple_gate[i]), bf(w_ple_proj[i]), tm=tm)
    return x


def kernel(x, p, positions, g_ffn1, w1_gate, w1_up, w1_down, g_mix, w_in, g_cq, w_uq, g_ckv, w_ukv, g_q_mla,
           g_k_mla, g_q_dil, g_k_dil, w_br_mla, w_br_dil, w_o, g_ffn2, w2_gate, w2_up, w2_down, g_ple,
           w_ple_gate, w_ple_proj):
    B, S, D = x.shape
    T = B * S
    pos = positions.reshape(T)
    tables = _rope_tables(pos)
    xt = x.reshape(T, D)
    for i in range(g_ffn1.shape[0]):
        xt = _layer(xt, p, i, g_ffn1, w1_gate, w1_up, w1_down, g_mix, w_in, g_cq, w_uq, g_ckv, w_ukv,
                    g_q_mla, g_k_mla, g_q_dil, g_k_dil, w_br_mla, w_br_dil, w_o, g_ffn2, w2_gate, w2_up,
                    w2_down, g_ple, w_ple_gate, w_ple_proj, tables, B, S)
    return xt.reshape(B, S, D)
```

```python
import functools

import jax
import jax.numpy as jnp
from jax import lax
from jax.experimental import pallas as pl
from jax.experimental.pallas import tpu as pltpu

F32 = jnp.float32
BF16 = jnp.bfloat16

EPS = 1e-6
NEG = -1e30
ROPE_THETA = 500000.0

LANES = 128
N_HEADS = 8
HEAD = 128
MLA_ROPE = 64
MLA_QK = HEAD + MLA_ROPE
MLA_PAD = 256
Q_LORA = 512
KV_LORA = 256
LAT_W = 1024
DIL_GROUPS = ((128, 1), (512, 4), (2048, 16))
N_GROUPS = len(DIL_GROUPS)
DIL_ROT = 32
BAND = 128
HW = N_HEADS * HEAD

VMEM_LIMIT = 56 * 1024 * 1024


def _cparams(*sem):
    return pltpu.CompilerParams(dimension_semantics=sem, vmem_limit_bytes=VMEM_LIMIT)


def _rmsnorm(x, g):
    ms = jnp.mean(x * x, axis=-1, keepdims=True)
    return x * lax.rsqrt(ms + EPS) * g


def _dot(a, b):
    return jnp.dot(a, b, preferred_element_type=F32)


def _dot_nt(a, b):
    return lax.dot_general(a, b, (((1,), (1,)), ((), ())), preferred_element_type=F32)


def _rope_tile(y, c, s):
    return y * c + pltpu.roll(y, LANES // 2, 1) * s


def _rope_pair(y, c, s1, s2, half):
    return y * c + pltpu.roll(y, LANES - half, 1) * s1 + pltpu.roll(y, half, 1) * s2


def _trig_body(pos_ref, inv_ref, cm_ref, sm_ref, cd_ref, sd1_ref, sd2_ref):
    hd, hm, mid = DIL_ROT // 2, MLA_ROPE // 2, LANES // 2
    ang = pos_ref[...].astype(F32) * inv_ref[...]
    cosv = jnp.cos(ang)
    sinv = jnp.sin(ang)
    lane = lax.broadcasted_iota(jnp.int32, ang.shape, 1)
    cd_ref[...] = jnp.where(lane < 2 * hd, cosv, 1.0)
    sd1_ref[...] = jnp.where(lane < hd, -sinv, 0.0)
    sd2_ref[...] = jnp.where(jnp.logical_and(lane >= hd, lane < 2 * hd), sinv, 0.0)
    cv = pltpu.roll(cosv, LANES - 2 * hd, 1)
    sv = pltpu.roll(sinv, LANES - 2 * hd, 1)
    x1 = lane < hm
    x2 = jnp.logical_and(lane >= mid, lane < mid + hm)
    cm_ref[...] = jnp.where(x1, cv, jnp.where(x2, cv, 1.0))
    sm_ref[...] = jnp.where(x1, -sv, jnp.where(x2, sv, 0.0))


def _rope_tables(pos):
    T = pos.shape[0]
    hd, hm = DIL_ROT // 2, MLA_ROPE // 2
    inv_d = ROPE_THETA ** (-jnp.arange(hd, dtype=F32) * 2.0 / (2 * hd))
    inv_m = ROPE_THETA ** (-jnp.arange(hm, dtype=F32) * 2.0 / (2 * hm))
    pad = jnp.zeros((LANES - 2 * hd - 2 * hm,), F32)
    inv = jnp.concatenate([inv_d, inv_d, inv_m, pad, inv_m]).reshape(1, LANES)
    tr = min(T, 512)
    tab = pl.BlockSpec((tr, LANES), lambda i: (i, 0))
    cm, sm, cd, sd1, sd2 = pl.pallas_call(
        _trig_body,
        grid=(T // tr,),
        in_specs=[pl.BlockSpec((tr, 1), lambda i: (i, 0)), pl.BlockSpec((1, LANES), lambda i: (0, 0))],
        out_specs=[tab] * 5,
        out_shape=[jax.ShapeDtypeStruct((T, LANES), F32)] * 5,
        compiler_params=_cparams("parallel"),
        name="rope_trig",
    )(pos.reshape(T, 1), inv)
    return (cm, sm), (cd, sd1, sd2)


def _ffn_body(x_ref, g_ref, wg_ref, wu_ref, wd_ref, o_ref, h_ref):
    f = pl.program_id(1)

    @pl.when(f == 0)
    def _():
        x = x_ref[...]
        h_ref[...] = _rmsnorm(x, g_ref[...]).astype(BF16)
        o_ref[...] = x

    h = h_ref[...]
    a = _dot(h, wg_ref[...])
    u = _dot(h, wu_ref[...])
    act = (0.5 * a) * jax.nn.sigmoid(a) * u
    o_ref[...] += _dot(act.astype(BF16), wd_ref[...])


def _ffn(x, g, wg, wu, wd, *, tm, tf):
    T, D = x.shape
    F = wg.shape[1]
    return pl.pallas_call(
        _ffn_body,
        grid=(T // tm, F // tf),
        in_specs=[
            pl.BlockSpec((tm, D), lambda i, f: (i, 0)),
            pl.BlockSpec((1, D), lambda i, f: (0, 0)),
            pl.BlockSpec((D, tf), lambda i, f: (0, f)),
            pl.BlockSpec((D, tf), lambda i, f: (0, f)),
            pl.BlockSpec((tf, D), lambda i, f: (f, 0)),
        ],
        out_specs=pl.BlockSpec((tm, D), lambda i, f: (i, 0)),
        out_shape=jax.ShapeDtypeStruct((T, D), F32),
        scratch_shapes=[pltpu.VMEM((tm, D), BF16)],
        compiler_params=_cparams("parallel", "arbitrary"),
        name="ffn",
    )(x, g, wg, wu, wd)


def _proj_body(x_ref, g_ref, w_ref, gd_ref, c_ref, s1_ref, s2_ref, lat_ref, d0_ref, d1_ref, d2_ref, gate_ref,
               h_ref, y_ref, z_ref, *, n_dil):
    j = pl.program_id(1)
    tm = y_ref.shape[0]

    @pl.when(j == 0)
    def _():
        h_ref[...] = _rmsnorm(x_ref[...], g_ref[...]).astype(BF16)

    y_ref[...] = _dot(h_ref[...], w_ref[...])

    @pl.when(j == 0)
    def _():
        lat_ref[...] = y_ref[...]

    is_dil = jnp.logical_and(j >= 1, j <= n_dil)
    is_v = (j - 1) % 3 == 2

    @pl.when(jnp.logical_and(is_dil, is_v))
    def _():
        for h in range(N_HEADS):
            z_ref[h] = y_ref[:, h * HEAD:(h + 1) * HEAD]

    @pl.when(jnp.logical_and(is_dil, jnp.logical_not(is_v)))
    def _():
        c = c_ref[...]
        s1 = s1_ref[...]
        s2 = s2_ref[...]
        pair = 2 * HEAD
        same_head = (lax.broadcasted_iota(jnp.int32, (pair, pair), 0) // HEAD
                     == lax.broadcasted_iota(jnp.int32, (pair, pair), 1) // HEAD)
        ones_bd = jnp.where(same_head, 1.0, 0.0).astype(BF16)
        for hp in range(N_HEADS // 2):
            sl = slice(hp * pair, (hp + 1) * pair)
            y = y_ref[:, sl]
            ss = _dot((y * y).astype(BF16), ones_bd)
            yn = y * lax.rsqrt(ss * (1.0 / HEAD) + EPS) * gd_ref[:, sl]
            for t in range(2):
                z_ref[2 * hp + t] = _rope_pair(yn[:, t * HEAD:(t + 1) * HEAD], c, s1, s2, DIL_ROT // 2)

    for g, d_ref in enumerate((d0_ref, d1_ref, d2_ref)):
        dil = DIL_GROUPS[g][1]

        @pl.when(jnp.logical_and(j >= 1 + 3 * g, j <= 3 + 3 * g))
        def _(d_ref=d_ref, dil=dil):
            for h in range(N_HEADS):
                for r in range(dil):
                    rows = pl.ds(r, tm // dil, stride=dil) if dil > 1 else slice(None)
                    d_ref[r, :, h * HEAD:(h + 1) * HEAD] = z_ref[h, rows, :].astype(BF16)

    @pl.when(j > n_dil)
    def _():
        gate_ref[...] = y_ref[...].astype(BF16)


def _in_proj(x, g, w, gd, c, s1, s2, *, tm, B, S):
    T, D = x.shape
    n_dil = 3 * N_GROUPS
    n_gate = (2 * D) // LAT_W
    nj = 1 + n_dil + n_gate
    assert w.shape[1] == nj * LAT_W
    per_b = S // tm

    def dil_spec(g):
        dil = DIL_GROUPS[g][1]
        return pl.BlockSpec((None, None, dil, tm // dil, HW),
                            lambda i, j: (jnp.clip(j - 1 - 3 * g, 0, 2), i // per_b, 0, i % per_b, 0))

    def dil_shape(g):
        dil = DIL_GROUPS[g][1]
        return jax.ShapeDtypeStruct((3, B, dil, S // dil, HW), BF16)

    tab = pl.BlockSpec((tm, LANES), lambda i, j: (i, 0))
    return pl.pallas_call(
        functools.partial(_proj_body, n_dil=n_dil),
        grid=(T // tm, nj),
        in_specs=[
            pl.BlockSpec((tm, D), lambda i, j: (i, 0)),
            pl.BlockSpec((1, D), lambda i, j: (0, 0)),
            pl.BlockSpec((D, LAT_W), lambda i, j: (0, j)),
            pl.BlockSpec((None, 1, HW), lambda i, j: (jnp.clip(j - 1, 0, n_dil - 1), 0, 0)),
            tab, tab, tab,
        ],
        out_specs=[
            pl.BlockSpec((tm, LAT_W), lambda i, j: (i, 0)),
            dil_spec(0), dil_spec(1), dil_spec(2),
            pl.BlockSpec((tm, LAT_W), lambda i, j: (i, jnp.clip(j - 1 - n_dil, 0, n_gate - 1))),
        ],
        out_shape=[
            jax.ShapeDtypeStruct((T, LAT_W), F32),
            dil_shape(0), dil_shape(1), dil_shape(2),
            jax.ShapeDtypeStruct((T, 2 * D), BF16),
        ],
        scratch_shapes=[pltpu.VMEM((tm, D), BF16), pltpu.VMEM((tm, LAT_W), F32),
                        pltpu.VMEM((N_HEADS, tm, HEAD), F32)],
        compiler_params=_cparams("parallel", "arbitrary"),
        name="in_proj",
    )(x, g, w, gd, c, s1, s2)


def _mla_prep_body(lat_ref, gcq_ref, gckv_ref, wuq_ref, wukv_ref, gqn_ref, gqr_ref, gkn_ref, gkr_ref,
                   c_ref, s_ref, q_ref, k_ref, v_ref):
    c = c_ref[...]
    s = s_ref[...]
    cq = _rmsnorm(lat_ref[:, :Q_LORA], gcq_ref[...]).astype(BF16)
    ckv = _rmsnorm(lat_ref[:, Q_LORA:Q_LORA + KV_LORA], gckv_ref[...]).astype(BF16)
    qa = _dot(cq, wuq_ref[...])
    kva = _dot(ckv, wukv_ref[...])
    kr = lat_ref[:, Q_LORA + KV_LORA:Q_LORA + KV_LORA + LANES]
    kr_ss = jnp.sum(kr * kr, axis=-1, keepdims=True)
    kr_rot = _rope_tile(kr * gkr_ref[...], c, s)
    for h in range(N_HEADS):
        lo = h * MLA_PAD
        qn = qa[:, lo:lo + HEAD]
        qr = qa[:, lo + HEAD:lo + MLA_PAD]
        ss = jnp.sum(qn * qn, axis=-1, keepdims=True) + jnp.sum(qr * qr, axis=-1, keepdims=True)
        r = lax.rsqrt(ss * (1.0 / MLA_QK) + EPS)
        q_ref[:, lo:lo + HEAD] = (qn * r * gqn_ref[...]).astype(BF16)
        q_ref[:, lo + HEAD:lo + MLA_PAD] = (_rope_tile(qr * gqr_ref[...], c, s) * r).astype(BF16)
        kn = kva[:, h * HEAD:(h + 1) * HEAD]
        ssk = jnp.sum(kn * kn, axis=-1, keepdims=True) + kr_ss
        rk = lax.rsqrt(ssk * (1.0 / MLA_QK) + EPS)
        k_ref[:, lo:lo + HEAD] = (kn * rk * gkn_ref[...]).astype(BF16)
        k_ref[:, lo + HEAD:lo + MLA_PAD] = (kr_rot * rk).astype(BF16)
    v_ref[...] = kva[:, HW:].astype(BF16)


def _mla_prep(lat, gcq, gckv, wuq, wukv, gqn, gqr, gkn, gkr, c, s, *, tm):
    T = lat.shape[0]
    row = lambda w: pl.BlockSpec((tm, w), lambda i: (i, 0))
    full = lambda a: pl.BlockSpec(a.shape, lambda i: (0, 0))
    return pl.pallas_call(
        _mla_prep_body,
        grid=(T // tm,),
        in_specs=[row(LAT_W), full(gcq), full(gckv), full(wuq), full(wukv), full(gqn), full(gqr), full(gkn),
                  full(gkr), row(LANES), row(LANES)],
        out_specs=[row(N_HEADS * MLA_PAD), row(N_HEADS * MLA_PAD), row(HW)],
        out_shape=[
            jax.ShapeDtypeStruct((T, N_HEADS * MLA_PAD), BF16),
            jax.ShapeDtypeStruct((T, N_HEADS * MLA_PAD), BF16),
            jax.ShapeDtypeStruct((T, HW), BF16),
        ],
        compiler_params=_cparams("parallel"),
        name="mla_prep",
    )(lat, gcq, gckv, wuq, wukv, gqn, gqr, gkn, gkr, c, s)


def _mla_attn_body(q_ref, k_ref, v_ref, o_ref, *, tq, tk):
    i = pl.program_id(2)
    q = q_ref[...]

    def update(kstart, carry, diag_offset):
        m, l, acc = carry
        s = _dot_nt(q, k_ref[pl.ds(kstart, tk), :])
        if diag_offset is not None:
            row = lax.broadcasted_iota(jnp.int32, s.shape, 0)
            col = lax.broadcasted_iota(jnp.int32, s.shape, 1) + diag_offset
            s = jnp.where(col <= row, s, NEG)
        m_new = jnp.maximum(m, jnp.max(s, axis=-1, keepdims=True))
        alpha = jnp.exp2(m - m_new)
        p = jnp.exp2(s - m_new)
        l = alpha * l + jnp.sum(p, axis=-1, keepdims=True)
        acc = alpha * acc + _dot(p.astype(BF16), v_ref[pl.ds(kstart, tk), :])
        return m_new, l, acc

    init = (jnp.full((tq, 1), NEG, F32), jnp.zeros((tq, 1), F32), jnp.zeros((tq, HEAD), F32))
    carry = lax.fori_loop(0, i * (tq // tk), lambda kc, c: update(pl.multiple_of(kc * tk, tk), c, None), init)
    for c in range(tq // tk):
        carry = update(pl.multiple_of(i * tq + c * tk, tk), carry, c * tk)
    _, l, acc = carry
    o_ref[...] = (acc / l).astype(BF16)


def _mla_attn(q, k, v, *, B, S, tq, tk):
    T = B * S
    nq = S // tq
    return pl.pallas_call(
        functools.partial(_mla_attn_body, tq=tq, tk=tk),
        grid=(B, N_HEADS, nq),
        in_specs=[
            pl.BlockSpec((tq, MLA_PAD), lambda b, h, i: (b * nq + i, h)),
            pl.BlockSpec((S, MLA_PAD), lambda b, h, i: (b, h)),
            pl.BlockSpec((S, HEAD), lambda b, h, i: (b, h)),
        ],
        out_specs=pl.BlockSpec((tq, HEAD), lambda b, h, i: (b * nq + i, h)),
        out_shape=jax.ShapeDtypeStruct((T, HW), BF16),
        compiler_params=_cparams("parallel", "parallel", "arbitrary"),
        name="mla_attn",
    )(q, k, v)


def _band_body(q_ref, k_ref, kh_ref, v_ref, vh_ref, o_ref, st_ref, o32_ref, *, dil, nsub):
    rb = pl.program_id(1)
    h = pl.program_id(2)
    row = lax.broadcasted_iota(jnp.int32, (BAND, BAND), 0)
    col = lax.broadcasted_iota(jnp.int32, (BAND, BAND), 1)
    bias_prev = jnp.where(col >= row, 0.0, NEG).astype(F32)
    bias_cur = jnp.where(col <= row, 0.0, NEG).astype(F32)
    lane = lax.broadcasted_iota(jnp.int32, (BAND, LANES), 1)
    is_m = lane == h
    is_l = lane == h + N_HEADS

    @pl.when(h == 0)
    def _():
        st_ref[...] = jnp.ones_like(st_ref)

    nb = dil * nsub

    def blocks(ref):
        return ref[...].reshape(nb, BAND, HEAD)

    def prev_blocks(ref, halo_ref):
        if nsub == 1:
            return halo_ref[...]
        shifted = jnp.concatenate([halo_ref[...], ref[:, :(nsub - 1) * BAND, :]], axis=1)
        return shifted.reshape(nb, BAND, HEAD)

    bdot_nt = lambda a, b: lax.dot_general(a, b, (((2,), (2,)), ((0,), (0,))), preferred_element_type=F32)
    bdot = lambda a, b: lax.dot_general(a, b, (((2,), (1,)), ((0,), (0,))), preferred_element_type=F32)
    q = blocks(q_ref)
    blk = lax.broadcasted_iota(jnp.int32, (nb, BAND, BAND), 0)
    first = jnp.logical_and(blk % nsub == 0, rb == 0)
    s_prev = bdot_nt(q, prev_blocks(k_ref, kh_ref)) + jnp.where(first, NEG, bias_prev[None])
    s_cur = bdot_nt(q, blocks(k_ref)) + bias_cur[None]
    m = jnp.max(jnp.maximum(s_prev, s_cur), axis=-1, keepdims=True)
    e_prev = jnp.exp2(s_prev - m)
    e_cur = jnp.exp2(s_cur - m)
    l = jnp.sum(e_prev + e_cur, axis=-1, keepdims=True)
    o = bdot(e_prev.astype(BF16), prev_blocks(v_ref, vh_ref)) + bdot(e_cur.astype(BF16), blocks(v_ref))
    o = o / l
    if dil == 1:
        o_ref[...] = o.reshape(nb * BAND, HEAD).astype(BF16)
    for b in range(nb):
        r, c = divmod(b, nsub)
        tok = pl.ds(c * BAND * dil + r, BAND, stride=dil) if dil > 1 else slice(c * BAND, (c + 1) * BAND)
        if dil > 1:
            o32_ref[tok, :] = o[b]
        st_ref[tok, :] = jnp.where(is_m, m[b], jnp.where(is_l, l[b], st_ref[tok, :]))
    if dil > 1:
        o_ref[...] = o32_ref[...].astype(BF16)


def _band_attn(slabs, g, *, B, S):
    dil = DIL_GROUPS[g][1]
    L = S // dil
    chunk = min(S, 2048)
    rb_rows = chunk // dil
    nsub = rb_rows // BAND
    nrb = L // rb_rows
    blk = lambda slab: pl.BlockSpec((None, None, dil, rb_rows, HEAD), lambda b, rb, h: (slab, b, 0, rb, h))
    halo = lambda slab: pl.BlockSpec(
        (None, None, dil, BAND, HEAD), lambda b, rb, h: (slab, b, 0, jnp.maximum(rb * nsub - 1, 0), h))
    T = B * S
    return pl.pallas_call(
        functools.partial(_band_body, dil=dil, nsub=nsub),
        grid=(B, nrb, N_HEADS),
        in_specs=[blk(0), blk(1), halo(1), blk(2), halo(2)],
        out_specs=[
            pl.BlockSpec((chunk, HEAD), lambda b, rb, h: (b * nrb + rb, h)),
            pl.BlockSpec((chunk, LANES), lambda b, rb, h: (b * nrb + rb, 0)),
        ],
        out_shape=[
            jax.ShapeDtypeStruct((T, HW), BF16),
            jax.ShapeDtypeStruct((T, LANES), F32),
        ],
        scratch_shapes=[pltpu.VMEM((chunk, HEAD), F32)],
        compiler_params=_cparams("parallel", "parallel", "arbitrary"),
        name=f"band_attn_{g}",
    )(slabs, slabs, slabs, slabs, slabs)


def _merge_body(x_ref, om_ref, o0_ref, o1_ref, o2_ref, st0_ref, st1_ref, st2_ref,
                ga_ref, gb_ref, wa_ref, wb_ref, wo_ref, out_ref, od_ref):
    f = pl.program_id(1)

    @pl.when(f == 0)
    def _():
        m0, m1, m2 = st0_ref[...], st1_ref[...], st2_ref[...]
        l0, l1, l2 = (pltpu.roll(st, LANES - N_HEADS, 1) for st in (m0, m1, m2))
        mx = jnp.maximum(jnp.maximum(m0, m1), m2)
        w0 = l0 * jnp.exp2(m0 - mx)
        w1 = l1 * jnp.exp2(m1 - mx)
        w2 = l2 * jnp.exp2(m2 - mx)
        tot = w0 + w1 + w2
        w0, w1, w2 = w0 / tot, w1 / tot, w2 / tot
        tm = w0.shape[0]
        for h in range(N_HEADS):
            hs = slice(h * HEAD, (h + 1) * HEAD)
            bc = lambda w: jnp.broadcast_to(w[:, h:h + 1], (tm, HEAD))
            od = (bc(w0) * o0_ref[:, hs].astype(F32) + bc(w1) * o1_ref[:, hs].astype(F32)
                  + bc(w2) * o2_ref[:, hs].astype(F32))
            od_ref[:, hs] = od.astype(BF16)
        out_ref[...] = x_ref[...]

    a = _dot(om_ref[...], wa_ref[...])
    b = _dot(od_ref[...], wb_ref[...])
    merged = jax.nn.sigmoid(ga_ref[...].astype(F32)) * a + jax.nn.sigmoid(gb_ref[...].astype(F32)) * b
    out_ref[...] += _dot(merged.astype(BF16), wo_ref[...])


def _merge(x, o_mla, o_dil, st_dil, gates, wa, wb, wo, *, tm, tf):
    T, D = x.shape
    nf = D // tf
    row = lambda w: pl.BlockSpec((tm, w), lambda i, f: (i, 0))
    return pl.pallas_call(
        _merge_body,
        grid=(T // tm, nf),
        in_specs=[row(D), row(HW), row(HW), row(HW), row(HW)] + [row(LANES)] * 3 + [
            pl.BlockSpec((tm, tf), lambda i, f: (i, f)),
            pl.BlockSpec((tm, tf), lambda i, f: (i, nf + f)),
            pl.BlockSpec((HW, tf), lambda i, f: (0, f)),
            pl.BlockSpec((HW, tf), lambda i, f: (0, f)),
            pl.BlockSpec((tf, D), lambda i, f: (f, 0)),
        ],
        out_specs=row(D),
        out_shape=jax.ShapeDtypeStruct((T, D), F32),
        scratch_shapes=[pltpu.VMEM((tm, HW), BF16)],
        compiler_params=_cparams("parallel", "arbitrary"),
        name="merge",
    )(x, o_mla, *o_dil, *st_dil, gates, gates, wa, wb, wo)


def _ple_body(x_ref, g_ref, p_ref, wg_ref, wp_ref, o_ref):
    x = x_ref[...]
    h = _rmsnorm(x, g_ref[...]).astype(BF16)
    gate = jax.nn.sigmoid(_dot(h, wg_ref[...]))
    o_ref[...] = x + gate * _dot(p_ref[...].astype(BF16), wp_ref[...])


def _ple(x, g, p, wg, wp, *, tm):
    T, D = x.shape
    P = p.shape[1]
    return pl.pallas_call(
        _ple_body,
        grid=(T // tm,),
        in_specs=[
            pl.BlockSpec((tm, D), lambda i: (i, 0)),
            pl.BlockSpec((1, D), lambda i: (0, 0)),
            pl.BlockSpec((tm, P), lambda i: (i, 0)),
            pl.BlockSpec((D, D), lambda i: (0, 0)),
            pl.BlockSpec((P, D), lambda i: (0, 0)),
        ],
        out_specs=pl.BlockSpec((tm, D), lambda i: (i, 0)),
        out_shape=jax.ShapeDtypeStruct((T, D), F32),
        compiler_params=_cparams("parallel"),
        name="ple",
    )(x, g, p, wg, wp)


def _spread_rope(a):
    half = MLA_ROPE // 2
    z = jnp.zeros(a.shape[:-1] + (LANES // 2 - half,), a.dtype)
    return jnp.concatenate([a[..., :half], z, a[..., half:], z], axis=-1)


def _layer(x, p, i, g_ffn1, w1_gate, w1_up, w1_down, g_mix, w_in, g_cq, w_uq, g_ckv, w_ukv, g_q_mla, g_k_mla,
           g_q_dil, g_k_dil, w_br_mla, w_br_dil, w_o, g_ffn2, w2_gate, w2_up, w2_down, g_ple, w_ple_gate,
           w_ple_proj, tables, B, S):
    T, D = x.shape
    tm = min(T, 512)
    tf = min(w1_gate.shape[-1], 512)
    bf = lambda a: a.astype(BF16)
    row = lambda a: a.reshape(1, -1).astype(F32)
    (c_mla, s_mla), tab_dil = tables
    log2e = 1.4426950408889634

    x = _ffn(x, row(g_ffn1[i]), bf(w1_gate[i]), bf(w1_up[i]), bf(w1_down[i]), tm=min(T, 1024), tf=tf)

    off_kr = Q_LORA + KV_LORA
    off_dil = off_kr + MLA_ROPE
    w = bf(w_in[i])
    w_r = jnp.concatenate([w[:, :off_kr], _spread_rope(w[:, off_kr:off_dil]),
                           jnp.zeros((D, LAT_W - off_kr - LANES), BF16), w[:, off_dil:]], axis=1)
    scale_dil = HEAD ** -0.5 * log2e
    tile_heads = lambda v: jnp.tile(v, N_HEADS).reshape(1, HW)
    gd = jnp.stack([tile_heads(t) for g in range(N_GROUPS)
                    for t in (g_q_dil[i, g] * scale_dil, g_k_dil[i, g], jnp.ones((HEAD,), F32))],
                   axis=0)
    lat, d0, d1, d2, gates = _in_proj(x, row(g_mix[i]), w_r, gd, *tab_dil, tm=tm, B=B, S=S)

    scale_mla = MLA_QK ** -0.5 * log2e
    wuq = w_uq[i].reshape(Q_LORA, N_HEADS, MLA_QK)
    wuq = jnp.concatenate([wuq[..., :HEAD], _spread_rope(wuq[..., HEAD:])], axis=-1)
    wuq = bf(wuq.reshape(Q_LORA, N_HEADS * MLA_PAD))
    wukv = bf(w_ukv[i].reshape(KV_LORA, N_HEADS, 2, HEAD).transpose(0, 2, 1, 3).reshape(KV_LORA, 2 * HW))
    gq, gk = g_q_mla[i] * scale_mla, g_k_mla[i]
    q, k, v = _mla_prep(lat, row(g_cq[i]), row(g_ckv[i]), wuq, wukv, row(gq[:HEAD]), row(_spread_rope(gq[HEAD:])),
                        row(gk[:HEAD]), row(_spread_rope(gk[HEAD:])), c_mla, s_mla, tm=tm)
    o_mla = _mla_attn(q, k, v, B=B, S=S, tq=min(S, 1024), tk=min(S, 1024))

    outs = [_band_attn(d, g, B=B, S=S) for g, d in enumerate((d0, d1, d2))]
    o_dil, st_dil = zip(*outs)

    x = _merge(x, o_mla, o_dil, st_dil, gates, bf(w_br_mla[i]), bf(w_br_dil[i]), bf(w_o[i]),
               tm=tm, tf=min(D, 1024))
    x = _ffn(x, row(g_ffn2[i]), bf(w2_gate[i]), bf(w2_up[i]), bf(w2_down[i]), tm=min(T, 1024), tf=tf)
    x = _ple(x, row(g_ple[i]), p[i].reshape(T, -1), bf(w_ple_gate[i]), bf(w_ple_proj[i]), tm=tm)
    return x


def kernel(x, p, positions, g_ffn1, w1_gate, w1_up, w1_down, g_mix, w_in, g_cq, w_uq, g_ckv, w_ukv, g_q_mla,
           g_k_mla, g_q_dil, g_k_dil, w_br_mla, w_br_dil, w_o, g_ffn2, w2_gate, w2_up, w2_down, g_ple,
           w_ple_gate, w_ple_proj):
    B, S, D = x.shape
    T = B * S
    pos = positions.reshape(T)
    tables = _rope_tables(pos)
    xt = x.reshape(T, D)
    for i in range(g_ffn1.shape[0]):
        xt = _layer(xt, p, i, g_ffn1, w1_gate, w1_up, w1_down, g_mix, w_in, g_cq, w_uq, g_ckv, w_ukv,
                    g_q_mla, g_k_mla, g_q_dil, g_k_dil, w_br_mla, w_br_dil, w_o, g_ffn2, w2_gate, w2_up,
                    w2_down, g_ple, w_ple_gate, w_ple_proj, tables, B, S)
    return xt.reshape(B, S, D)
```

```python
import functools

import jax
import jax.numpy as jnp
from jax import lax
from jax.experimental import pallas as pl
from jax.experimental.pallas import tpu as pltpu

F32 = jnp.float32
BF16 = jnp.bfloat16

EPS = 1e-6
NEG = -1e30
ROPE_THETA = 500000.0

LANES = 128
N_HEADS = 8
HEAD = 128
MLA_ROPE = 64
MLA_QK = HEAD + MLA_ROPE
MLA_PAD = 256
Q_LORA = 512
KV_LORA = 256
LAT_W = 1024
DIL_GROUPS = ((128, 1), (512, 4), (2048, 16))
N_GROUPS = len(DIL_GROUPS)
DIL_ROT = 32
BAND = 128
HW = N_HEADS * HEAD

VMEM_LIMIT = 56 * 1024 * 1024


def _cparams(*sem):
    return pltpu.CompilerParams(dimension_semantics=sem, vmem_limit_bytes=VMEM_LIMIT)


def _rmsnorm(x, g):
    ms = jnp.mean(x * x, axis=-1, keepdims=True)
    return x * lax.rsqrt(ms + EPS) * g


def _dot(a, b):
    return jnp.dot(a, b, preferred_element_type=F32)


def _dot_nt(a, b):
    return lax.dot_general(a, b, (((1,), (1,)), ((), ())), preferred_element_type=F32)


def _rope_tile(y, c, s):
    return y * c + pltpu.roll(y, LANES // 2, 1) * s


def _rope_pair(y, c, s1, s2, half):
    return y * c + pltpu.roll(y, LANES - half, 1) * s1 + pltpu.roll(y, half, 1) * s2


def _trig_body(pos_ref, inv_ref, cm_ref, sm_ref, cd_ref, sd1_ref, sd2_ref):
    hd, hm, mid = DIL_ROT // 2, MLA_ROPE // 2, LANES // 2
    ang = pos_ref[...].astype(F32) * inv_ref[...]
    cosv = jnp.cos(ang)
    sinv = jnp.sin(ang)
    lane = lax.broadcasted_iota(jnp.int32, ang.shape, 1)
    cd_ref[...] = jnp.where(lane < 2 * hd, cosv, 1.0)
    sd1_ref[...] = jnp.where(lane < hd, -sinv, 0.0)
    sd2_ref[...] = jnp.where(jnp.logical_and(lane >= hd, lane < 2 * hd), sinv, 0.0)
    cv = pltpu.roll(cosv, LANES - 2 * hd, 1)
    sv = pltpu.roll(sinv, LANES - 2 * hd, 1)
    x1 = lane < hm
    x2 = jnp.logical_and(lane >= mid, lane < mid + hm)
    cm_ref[...] = jnp.where(x1, cv, jnp.where(x2, cv, 1.0))
    sm_ref[...] = jnp.where(x1, -sv, jnp.where(x2, sv, 0.0))


def _rope_tables(pos):
    T = pos.shape[0]
    hd, hm = DIL_ROT // 2, MLA_ROPE // 2
    inv_d = ROPE_THETA ** (-jnp.arange(hd, dtype=F32) * 2.0 / (2 * hd))
    inv_m = ROPE_THETA ** (-jnp.arange(hm, dtype=F32) * 2.0 / (2 * hm))
    pad = jnp.zeros((LANES - 2 * hd - 2 * hm,), F32)
    inv = jnp.concatenate([inv_d, inv_d, inv_m, pad, inv_m]).reshape(1, LANES)
    tr = min(T, 512)
    tab = pl.BlockSpec((tr, LANES), lambda i: (i, 0))
    cm, sm, cd, sd1, sd2 = pl.pallas_call(
        _trig_body,
        grid=(T // tr,),
        in_specs=[pl.BlockSpec((tr, 1), lambda i: (i, 0)), pl.BlockSpec((1, LANES), lambda i: (0, 0))],
        out_specs=[tab] * 5,
        out_shape=[jax.ShapeDtypeStruct((T, LANES), F32)] * 5,
        compiler_params=_cparams("parallel"),
        name="rope_trig",
    )(pos.reshape(T, 1), inv)
    return (cm, sm), (cd, sd1, sd2)


def _ffn_body(x_ref, g_ref, wg_ref, wu_ref, wd_ref, o_ref, h_ref):
    f = pl.program_id(1)

    @pl.when(f == 0)
    def _():
        x = x_ref[...]
        h_ref[...] = _rmsnorm(x, g_ref[...]).astype(BF16)
        o_ref[...] = x

    h = h_ref[...]
    a = _dot(h, wg_ref[...])
    u = _dot(h, wu_ref[...])
    act = (0.5 * a) * jax.nn.sigmoid(a) * u
    o_ref[...] += _dot(act.astype(BF16), wd_ref[...])


def _ffn(x, g, wg, wu, wd, *, tm, tf):
    T, D = x.shape
    F = wg.shape[1]
    return pl.pallas_call(
        _ffn_body,
        grid=(T // tm, F // tf),
        in_specs=[
            pl.BlockSpec((tm, D), lambda i, f: (i, 0)),
            pl.BlockSpec((1, D), lambda i, f: (0, 0)),
            pl.BlockSpec((D, tf), lambda i, f: (0, f)),
            pl.BlockSpec((D, tf), lambda i, f: (0, f)),
            pl.BlockSpec((tf, D), lambda i, f: (f, 0)),
        ],
        out_specs=pl.BlockSpec((tm, D), lambda i, f: (i, 0)),
        out_shape=jax.ShapeDtypeStruct((T, D), F32),
        scratch_shapes=[pltpu.VMEM((tm, D), BF16)],
        compiler_params=_cparams("parallel", "arbitrary"),
        name="ffn",
    )(x, g, wg, wu, wd)


def _proj_body(x_ref, g_ref, w_ref, gd_ref, c_ref, s1_ref, s2_ref, lat_ref, d0_ref, d1_ref, d2_ref, gate_ref,
               h_ref, y_ref, z_ref, *, n_dil):
    j = pl.program_id(1)
    tm = y_ref.shape[0]

    @pl.when(j == 0)
    def _():
        h_ref[...] = _rmsnorm(x_ref[...], g_ref[...]).astype(BF16)

    y_ref[...] = _dot(h_ref[...], w_ref[...])

    @pl.when(j == 0)
    def _():
        lat_ref[...] = y_ref[...]

    is_dil = jnp.logical_and(j >= 1, j <= n_dil)
    is_v = (j - 1) % 3 == 2

    @pl.when(jnp.logical_and(is_dil, is_v))
    def _():
        for h in range(N_HEADS):
            z_ref[h] = y_ref[:, h * HEAD:(h + 1) * HEAD]

    @pl.when(jnp.logical_and(is_dil, jnp.logical_not(is_v)))
    def _():
        c = c_ref[...]
        s1 = s1_ref[...]
        s2 = s2_ref[...]
        pair = 2 * HEAD
        same_head = (lax.broadcasted_iota(jnp.int32, (pair, pair), 0) // HEAD
                     == lax.broadcasted_iota(jnp.int32, (pair, pair), 1) // HEAD)
        ones_bd = jnp.where(same_head, 1.0, 0.0).astype(BF16)
        half = DIL_ROT // 2
        src = lax.broadcasted_iota(jnp.int32, (pair, pair), 0)
        dst = lax.broadcasted_iota(jnp.int32, (pair, pair), 1)
        dst_in_head = dst % HEAD
        partner = jnp.where(dst_in_head < half, dst + half, jnp.where(dst_in_head < 2 * half, dst - half, -1))
        swap_bd = jnp.where(src == partner, 1.0, 0.0).astype(BF16)
        s = s1 + s2
        for hp in range(N_HEADS // 2):
            sl = slice(hp * pair, (hp + 1) * pair)
            y = y_ref[:, sl]
            ss = _dot((y * y).astype(BF16), ones_bd)
            yn = y * lax.rsqrt(ss * (1.0 / HEAD) + EPS) * gd_ref[:, sl]
            swapped = _dot(yn.astype(BF16), swap_bd)
            for t in range(2):
                ht = slice(t * HEAD, (t + 1) * HEAD)
                z_ref[2 * hp + t] = yn[:, ht] * c + swapped[:, ht] * s

    for g, d_ref in enumerate((d0_ref, d1_ref, d2_ref)):
        dil = DIL_GROUPS[g][1]

        @pl.when(jnp.logical_and(j >= 1 + 3 * g, j <= 3 + 3 * g))
        def _(d_ref=d_ref, dil=dil):
            for h in range(N_HEADS):
                for r in range(dil):
                    rows = pl.ds(r, tm // dil, stride=dil) if dil > 1 else slice(None)
                    d_ref[r, :, h * HEAD:(h + 1) * HEAD] = z_ref[h, rows, :].astype(BF16)

    @pl.when(j > n_dil)
    def _():
        gate_ref[...] = y_ref[...].astype(BF16)


def _in_proj(x, g, w, gd, c, s1, s2, *, tm, B, S):
    T, D = x.shape
    n_dil = 3 * N_GROUPS
    n_gate = (2 * D) // LAT_W
    nj = 1 + n_dil + n_gate
    assert w.shape[1] == nj * LAT_W
    per_b = S // tm

    def dil_spec(g):
        dil = DIL_GROUPS[g][1]
        return pl.BlockSpec((None, None, dil, tm // dil, HW),
                            lambda i, j: (jnp.clip(j - 1 - 3 * g, 0, 2), i // per_b, 0, i % per_b, 0))

    def dil_shape(g):
        dil = DIL_GROUPS[g][1]
        return jax.ShapeDtypeStruct((3, B, dil, S // dil, HW), BF16)

    tab = pl.BlockSpec((tm, LANES), lambda i, j: (i, 0))
    return pl.pallas_call(
        functools.partial(_proj_body, n_dil=n_dil),
        grid=(T // tm, nj),
        in_specs=[
            pl.BlockSpec((tm, D), lambda i, j: (i, 0)),
            pl.BlockSpec((1, D), lambda i, j: (0, 0)),
            pl.BlockSpec((D, LAT_W), lambda i, j: (0, j)),
            pl.BlockSpec((None, 1, HW), lambda i, j: (jnp.clip(j - 1, 0, n_dil - 1), 0, 0)),
            tab, tab, tab,
        ],
        out_specs=[
            pl.BlockSpec((tm, LAT_W), lambda i, j: (i, 0)),
            dil_spec(0), dil_spec(1), dil_spec(2),
            pl.BlockSpec((tm, LAT_W), lambda i, j: (i, jnp.clip(j - 1 - n_dil, 0, n_gate - 1))),
        ],
        out_shape=[
            jax.ShapeDtypeStruct((T, LAT_W), F32),
            dil_shape(0), dil_shape(1), dil_shape(2),
            jax.ShapeDtypeStruct((T, 2 * D), BF16),
        ],
        scratch_shapes=[pltpu.VMEM((tm, D), BF16), pltpu.VMEM((tm, LAT_W), F32),
                        pltpu.VMEM((N_HEADS, tm, HEAD), F32)],
        compiler_params=_cparams("parallel", "arbitrary"),
        name="in_proj",
    )(x, g, w, gd, c, s1, s2)


def _mla_prep_body(lat_ref, gcq_ref, gckv_ref, wuq_ref, wukv_ref, gqn_ref, gqr_ref, gkn_ref, gkr_ref,
                   c_ref, s_ref, q_ref, k_ref, v_ref):
    c = c_ref[...]
    s = s_ref[...]
    cq = _rmsnorm(lat_ref[:, :Q_LORA], gcq_ref[...]).astype(BF16)
    ckv = _rmsnorm(lat_ref[:, Q_LORA:Q_LORA + KV_LORA], gckv_ref[...]).astype(BF16)
    qa = _dot(cq, wuq_ref[...])
    kva = _dot(ckv, wukv_ref[...])
    kr = lat_ref[:, Q_LORA + KV_LORA:Q_LORA + KV_LORA + LANES]
    kr_ss = jnp.sum(kr * kr, axis=-1, keepdims=True)
    kr_rot = _rope_tile(kr * gkr_ref[...], c, s)
    for h in range(N_HEADS):
        lo = h * MLA_PAD
        qn = qa[:, lo:lo + HEAD]
        qr = qa[:, lo + HEAD:lo + MLA_PAD]
        ss = jnp.sum(qn * qn, axis=-1, keepdims=True) + jnp.sum(qr * qr, axis=-1, keepdims=True)
        r = lax.rsqrt(ss * (1.0 / MLA_QK) + EPS)
        q_ref[:, lo:lo + HEAD] = (qn * r * gqn_ref[...]).astype(BF16)
        q_ref[:, lo + HEAD:lo + MLA_PAD] = (_rope_tile(qr * gqr_ref[...], c, s) * r).astype(BF16)
        kn = kva[:, h * HEAD:(h + 1) * HEAD]
        ssk = jnp.sum(kn * kn, axis=-1, keepdims=True) + kr_ss
        rk = lax.rsqrt(ssk * (1.0 / MLA_QK) + EPS)
        k_ref[:, lo:lo + HEAD] = (kn * rk * gkn_ref[...]).astype(BF16)
        k_ref[:, lo + HEAD:lo + MLA_PAD] = (kr_rot * rk).astype(BF16)
    v_ref[...] = kva[:, HW:].astype(BF16)


def _mla_prep(lat, gcq, gckv, wuq, wukv, gqn, gqr, gkn, gkr, c, s, *, tm):
    T = lat.shape[0]
    row = lambda w: pl.BlockSpec((tm, w), lambda i: (i, 0))
    full = lambda a: pl.BlockSpec(a.shape, lambda i: (0, 0))
    return pl.pallas_call(
        _mla_prep_body,
        grid=(T // tm,),
        in_specs=[row(LAT_W), full(gcq), full(gckv), full(wuq), full(wukv), full(gqn), full(gqr), full(gkn),
                  full(gkr), row(LANES), row(LANES)],
        out_specs=[row(N_HEADS * MLA_PAD), row(N_HEADS * MLA_PAD), row(HW)],
        out_shape=[
            jax.ShapeDtypeStruct((T, N_HEADS * MLA_PAD), BF16),
            jax.ShapeDtypeStruct((T, N_HEADS * MLA_PAD), BF16),
            jax.ShapeDtypeStruct((T, HW), BF16),
        ],
        compiler_params=_cparams("parallel"),
        name="mla_prep",
    )(lat, gcq, gckv, wuq, wukv, gqn, gqr, gkn, gkr, c, s)


def _mla_attn_body(q_ref, k_ref, v_ref, o_ref, *, tq, tk):
    i = pl.program_id(2)
    q = q_ref[...]

    def update(kstart, carry, diag_offset):
        m, l, acc = carry
        s = _dot_nt(q, k_ref[pl.ds(kstart, tk), :])
        if diag_offset is not None:
            row = lax.broadcasted_iota(jnp.int32, s.shape, 0)
            col = lax.broadcasted_iota(jnp.int32, s.shape, 1) + diag_offset
            s = jnp.where(col <= row, s, NEG)
        m_new = jnp.maximum(m, jnp.max(s, axis=-1, keepdims=True))
        alpha = jnp.exp2(m - m_new)
        p = jnp.exp2(s - m_new)
        l = alpha * l + jnp.sum(p, axis=-1, keepdims=True)
        acc = alpha * acc + _dot(p.astype(BF16), v_ref[pl.ds(kstart, tk), :])
        return m_new, l, acc

    init = (jnp.full((tq, 1), NEG, F32), jnp.zeros((tq, 1), F32), jnp.zeros((tq, HEAD), F32))
    carry = lax.fori_loop(0, i * (tq // tk), lambda kc, c: update(pl.multiple_of(kc * tk, tk), c, None), init)
    for c in range(tq // tk):
        carry = update(pl.multiple_of(i * tq + c * tk, tk), carry, c * tk)
    _, l, acc = carry
    o_ref[...] = (acc / l).astype(BF16)


def _mla_attn(q, k, v, *, B, S, tq, tk):
    T = B * S
    nq = S // tq
    return pl.pallas_call(
        functools.partial(_mla_attn_body, tq=tq, tk=tk),
        grid=(B, N_HEADS, nq),
        in_specs=[
            pl.BlockSpec((tq, MLA_PAD), lambda b, h, i: (b * nq + i, h)),
            pl.BlockSpec((S, MLA_PAD), lambda b, h, i: (b, h)),
            pl.BlockSpec((S, HEAD), lambda b, h, i: (b, h)),
        ],
        out_specs=pl.BlockSpec((tq, HEAD), lambda b, h, i: (b * nq + i, h)),
        out_shape=jax.ShapeDtypeStruct((T, HW), BF16),
        compiler_params=_cparams("parallel", "parallel", "arbitrary"),
        name="mla_attn",
    )(q, k, v)


def _band_body(q_ref, k_ref, kh_ref, v_ref, vh_ref, o_ref, st_ref, o32_ref, *, dil, nsub):
    rb = pl.program_id(1)
    h = pl.program_id(2)
    row = lax.broadcasted_iota(jnp.int32, (BAND, BAND), 0)
    col = lax.broadcasted_iota(jnp.int32, (BAND, BAND), 1)
    bias_prev = jnp.where(col >= row, 0.0, NEG).astype(F32)
    bias_cur = jnp.where(col <= row, 0.0, NEG).astype(F32)
    lane = lax.broadcasted_iota(jnp.int32, (BAND, LANES), 1)
    is_m = lane == h
    is_l = lane == h + N_HEADS

    @pl.when(h == 0)
    def _():
        st_ref[...] = jnp.ones_like(st_ref)

    nb = dil * nsub

    def blocks(ref):
        return ref[...].reshape(nb, BAND, HEAD)

    def prev_blocks(ref, halo_ref):
        if nsub == 1:
            return halo_ref[...]
        shifted = jnp.concatenate([halo_ref[...], ref[:, :(nsub - 1) * BAND, :]], axis=1)
        return shifted.reshape(nb, BAND, HEAD)

    bdot_nt = lambda a, b: lax.dot_general(a, b, (((2,), (2,)), ((0,), (0,))), preferred_element_type=F32)
    bdot = lambda a, b: lax.dot_general(a, b, (((2,), (1,)), ((0,), (0,))), preferred_element_type=F32)
    q = blocks(q_ref)
    blk = lax.broadcasted_iota(jnp.int32, (nb, BAND, BAND), 0)
    first = jnp.logical_and(blk % nsub == 0, rb == 0)
    s_prev = bdot_nt(q, prev_blocks(k_ref, kh_ref)) + jnp.where(first, NEG, bias_prev[None])
    s_cur = bdot_nt(q, blocks(k_ref)) + bias_cur[None]
    m = jnp.max(jnp.maximum(s_prev, s_cur), axis=-1, keepdims=True)
    e_prev = jnp.exp2(s_prev - m)
    e_cur = jnp.exp2(s_cur - m)
    l = jnp.sum(e_prev + e_cur, axis=-1, keepdims=True)
    o = bdot(e_prev.astype(BF16), prev_blocks(v_ref, vh_ref)) + bdot(e_cur.astype(BF16), blocks(v_ref))
    o = o / l
    if dil == 1:
        o_ref[...] = o.reshape(nb * BAND, HEAD).astype(BF16)
    for b in range(nb):
        r, c = divmod(b, nsub)
        tok = pl.ds(c * BAND * dil + r, BAND, stride=dil) if dil > 1 else slice(c * BAND, (c + 1) * BAND)
        if dil > 1:
            o32_ref[tok, :] = o[b]
        st_ref[tok, :] = jnp.where(is_m, m[b], jnp.where(is_l, l[b], st_ref[tok, :]))
    if dil > 1:
        o_ref[...] = o32_ref[...].astype(BF16)


def _band_attn(slabs, g, *, B, S):
    dil = DIL_GROUPS[g][1]
    L = S // dil
    chunk = min(S, 2048)
    rb_rows = chunk // dil
    nsub = rb_rows // BAND
    nrb = L // rb_rows
    blk = lambda slab: pl.BlockSpec((None, None, dil, rb_rows, HEAD), lambda b, rb, h: (slab, b, 0, rb, h))
    halo = lambda slab: pl.BlockSpec(
        (None, None, dil, BAND, HEAD), lambda b, rb, h: (slab, b, 0, jnp.maximum(rb * nsub - 1, 0), h))
    T = B * S
    return pl.pallas_call(
        functools.partial(_band_body, dil=dil, nsub=nsub),
        grid=(B, nrb, N_HEADS),
        in_specs=[blk(0), blk(1), halo(1), blk(2), halo(2)],
        out_specs=[
            pl.BlockSpec((chunk, HEAD), lambda b, rb, h: (b * nrb + rb, h)),
            pl.BlockSpec((chunk, LANES), lambda b, rb, h: (b * nrb + rb, 0)),
        ],
        out_shape=[
            jax.ShapeDtypeStruct((T, HW), BF16),
            jax.ShapeDtypeStruct((T, LANES), F32),
        ],
        scratch_shapes=[pltpu.VMEM((chunk, HEAD), F32)],
        compiler_params=_cparams("parallel", "parallel", "arbitrary"),
        name=f"band_attn_{g}",
    )(slabs, slabs, slabs, slabs, slabs)


def _merge_body(x_ref, om_ref, o0_ref, o1_ref, o2_ref, st0_ref, st1_ref, st2_ref,
                ga_ref, gb_ref, wa_ref, wb_ref, wo_ref, out_ref, od_ref):
    f = pl.program_id(1)

    @pl.when(f == 0)
    def _():
        m0, m1, m2 = st0_ref[...], st1_ref[...], st2_ref[...]
        l0, l1, l2 = (pltpu.roll(st, LANES - N_HEADS, 1) for st in (m0, m1, m2))
        mx = jnp.maximum(jnp.maximum(m0, m1), m2)
        w0 = l0 * jnp.exp2(m0 - mx)
        w1 = l1 * jnp.exp2(m1 - mx)
        w2 = l2 * jnp.exp2(m2 - mx)
        tot = w0 + w1 + w2
        w0, w1, w2 = w0 / tot, w1 / tot, w2 / tot
        tm = w0.shape[0]
        for h in range(N_HEADS):
            hs = slice(h * HEAD, (h + 1) * HEAD)
            bc = lambda w: jnp.broadcast_to(w[:, h:h + 1], (tm, HEAD))
            od = (bc(w0) * o0_ref[:, hs].astype(F32) + bc(w1) * o1_ref[:, hs].astype(F32)
                  + bc(w2) * o2_ref[:, hs].astype(F32))
            od_ref[:, hs] = od.astype(BF16)
        out_ref[...] = x_ref[...]

    a = _dot(om_ref[...], wa_ref[...])
    b = _dot(od_ref[...], wb_ref[...])
    merged = jax.nn.sigmoid(ga_ref[...].astype(F32)) * a + jax.nn.sigmoid(gb_ref[...].astype(F32)) * b
    out_ref[...] += _dot(merged.astype(BF16), wo_ref[...])


def _merge(x, o_mla, o_dil, st_dil, gates, wa, wb, wo, *, tm, tf):
    T, D = x.shape
    nf = D // tf
    row = lambda w: pl.BlockSpec((tm, w), lambda i, f: (i, 0))
    return pl.pallas_call(
        _merge_body,
        grid=(T // tm, nf),
        in_specs=[row(D), row(HW), row(HW), row(HW), row(HW)] + [row(LANES)] * 3 + [
            pl.BlockSpec((tm, tf), lambda i, f: (i, f)),
            pl.BlockSpec((tm, tf), lambda i, f: (i, nf + f)),
            pl.BlockSpec((HW, tf), lambda i, f: (0, f)),
            pl.BlockSpec((HW, tf), lambda i, f: (0, f)),
            pl.BlockSpec((tf, D), lambda i, f: (f, 0)),
        ],
        out_specs=row(D),
        out_shape=jax.ShapeDtypeStruct((T, D), F32),
        scratch_shapes=[pltpu.VMEM((tm, HW), BF16)],
        compiler_params=_cparams("parallel", "arbitrary"),
        name="merge",
    )(x, o_mla, *o_dil, *st_dil, gates, gates, wa, wb, wo)


def _ple_body(x_ref, g_ref, p_ref, wg_ref, wp_ref, o_ref):
    x = x_ref[...]
    h = _rmsnorm(x, g_ref[...]).astype(BF16)
    gate = jax.nn.sigmoid(_dot(h, wg_ref[...]))
    o_ref[...] = x + gate * _dot(p_ref[...].astype(BF16), wp_ref[...])


def _ple(x, g, p, wg, wp, *, tm):
    T, D = x.shape
    P = p.shape[1]
    return pl.pallas_call(
        _ple_body,
        grid=(T // tm,),
        in_specs=[
            pl.BlockSpec((tm, D), lambda i: (i, 0)),
            pl.BlockSpec((1, D), lambda i: (0, 0)),
            pl.BlockSpec((tm, P), lambda i: (i, 0)),
            pl.BlockSpec((D, D), lambda i: (0, 0)),
            pl.BlockSpec((P, D), lambda i: (0, 0)),
        ],
        out_specs=pl.BlockSpec((tm, D), lambda i: (i, 0)),
        out_shape=jax.ShapeDtypeStruct((T, D), F32),
        compiler_params=_cparams("parallel"),
        name="ple",
    )(x, g, p, wg, wp)


def _spread_rope(a):
    half = MLA_ROPE // 2
    z = jnp.zeros(a.shape[:-1] + (LANES // 2 - half,), a.dtype)
    return jnp.concatenate([a[..., :half], z, a[..., half:], z], axis=-1)


def _layer(x, p, i, g_ffn1, w1_gate, w1_up, w1_down, g_mix, w_in, g_cq, w_uq, g_ckv, w_ukv, g_q_mla, g_k_mla,
           g_q_dil, g_k_dil, w_br_mla, w_br_dil, w_o, g_ffn2, w2_gate, w2_up, w2_down, g_ple, w_ple_gate,
           w_ple_proj, tables, B, S):
    T, D = x.shape
    tm = min(T, 512)
    tf = min(w1_gate.shape[-1], 512)
    bf = lambda a: a.astype(BF16)
    row = lambda a: a.reshape(1, -1).astype(F32)
    (c_mla, s_mla), tab_dil = tables
    log2e = 1.4426950408889634

    x = _ffn(x, row(g_ffn1[i]), bf(w1_gate[i]), bf(w1_up[i]), bf(w1_down[i]), tm=min(T, 1024), tf=tf)

    off_kr = Q_LORA + KV_LORA
    off_dil = off_kr + MLA_ROPE
    w = bf(w_in[i])
    w_r = jnp.concatenate([w[:, :off_kr], _spread_rope(w[:, off_kr:off_dil]),
                           jnp.zeros((D, LAT_W - off_kr - LANES), BF16), w[:, off_dil:]], axis=1)
    scale_dil = HEAD ** -0.5 * log2e
    tile_heads = lambda v: jnp.tile(v, N_HEADS).reshape(1, HW)
    gd = jnp.stack([tile_heads(t) for g in range(N_GROUPS)
                    for t in (g_q_dil[i, g] * scale_dil, g_k_dil[i, g], jnp.ones((HEAD,), F32))],
                   axis=0)
    lat, d0, d1, d2, gates = _in_proj(x, row(g_mix[i]), w_r, gd, *tab_dil, tm=tm, B=B, S=S)

    scale_mla = MLA_QK ** -0.5 * log2e
    wuq = w_uq[i].reshape(Q_LORA, N_HEADS, MLA_QK)
    wuq = jnp.concatenate([wuq[..., :HEAD], _spread_rope(wuq[..., HEAD:])], axis=-1)
    wuq = bf(wuq.reshape(Q_LORA, N_HEADS * MLA_PAD))
    wukv = bf(w_ukv[i].reshape(KV_LORA, N_HEADS, 2, HEAD).transpose(0, 2, 1, 3).reshape(KV_LORA, 2 * HW))
    gq, gk = g_q_mla[i] * scale_mla, g_k_mla[i]
    q, k, v = _mla_prep(lat, row(g_cq[i]), row(g_ckv[i]), wuq, wukv, row(gq[:HEAD]), row(_spread_rope(gq[HEAD:])),
                        row(gk[:HEAD]), row(_spread_rope(gk[HEAD:])), c_mla, s_mla, tm=tm)
    o_mla = _mla_attn(q, k, v, B=B, S=S, tq=min(S, 1024), tk=min(S, 1024))

    outs = [_band_attn(d, g, B=B, S=S) for g, d in enumerate((d0, d1, d2))]
    o_dil, st_dil = zip(*outs)

    x = _merge(x, o_mla, o_dil, st_dil, gates, bf(w_br_mla[i]), bf(w_br_dil[i]), bf(w_o[i]),
               tm=tm, tf=min(D, 1024))
    x = _ffn(x, row(g_ffn2[i]), bf(w2_gate[i]), bf(w2_up[i]), bf(w2_down[i]), tm=min(T, 1024), tf=tf)
    x = _ple(x, row(g_ple[i]), p[i].reshape(T, -1), bf(w_ple_gate[i]), bf(w_ple_proj[i]), tm=tm)
    return x


def kernel(x, p, positions, g_ffn1, w1_gate, w1_up, w1_down, g_mix, w_in, g_cq, w_uq, g_ckv, w_ukv, g_q_mla,
           g_k_mla, g_q_dil, g_k_dil, w_br_mla, w_br_dil, w_o, g_ffn2, w2_gate, w2_up, w2_down, g_ple,
           w_ple_gate, w_ple_proj):
    B, S, D = x.shape
    T = B * S
    pos = positions.reshape(T)
    tables = _rope_tables(pos)
    xt = x.reshape(T, D)
    for i in range(g_ffn1.shape[0]):
        xt = _layer(xt, p, i, g_ffn1, w1_gate, w1_up, w1_down, g_mix, w_in, g_cq, w_uq, g_ckv, w_ukv,
                    g_q_mla, g_k_mla, g_q_dil, g_k_dil, w_br_mla, w_br_dil, w_o, g_ffn2, w2_gate, w2_up,
                    w2_down, g_ple, w_ple_gate, w_ple_proj, tables, B, S)
    return xt.reshape(B, S, D)
```
